```python
import jax, jax.numpy as jnp
from jax import lax
import numpy as np

D_MODEL = 2048
BATCH = 4
SEQ = 2048
DEPTH = 2
DEC_BATCH = 128
DEC_SEQ = 4
PAST_LEN = 2048
PAGE_SIZE = 128

N_MIXERS = 2
N_LAYERS_A = (DEPTH + 1) // 2
N_LAYERS_B = DEPTH // 2

A_HEADS = 8
A_DV = D_MODEL // A_HEADS
A_DK = A_DV // 2
A_CHUNK = 64
A_QK_W = A_HEADS * A_DK
A_V_W = A_HEADS * A_DV
A_IN_W = 2 * A_QK_W + 2 * A_V_W + 2 * A_HEADS

B_GROUPS = ((128, 1), (512, 4), (2048, 16))
N_GROUPS_B = len(B_GROUPS)
B_HEADS = 16
B_DH = D_MODEL // B_HEADS
B_QKV_W = 3 * N_GROUPS_B * B_HEADS * B_DH
ROPE_THETA = 10000.0

MOE_GROUPS = 4
MOE_PER_GROUP = 4
N_EXPERTS = MOE_GROUPS * MOE_PER_GROUP
MOE_TOPK = 2
D_FF_EXPERT = 1024

EPS = 1e-6

kernel_name = 'hybrid_mlstm_dilated_hmoe_step'


def rmsnorm(x, g):
    xf = x.astype(jnp.float32)
    y = xf * lax.rsqrt(jnp.mean(xf * xf, axis=-1, keepdims=True) + EPS)
    return (y * g.astype(jnp.float32)).astype(x.dtype)


def rotary(x, pos):
    half = x.shape[-1] // 2
    inv_freq = ROPE_THETA ** (-jnp.arange(half, dtype=jnp.float32) / half)
    ang = pos.astype(jnp.float32)[:, None] * inv_freq[None, :]
    bshape = (ang.shape[0],) + (1,) * (x.ndim - 3) + (half,)
    cos = jnp.cos(ang).reshape(bshape)
    sin = jnp.sin(ang).reshape(bshape)
    xf = x.astype(jnp.float32)
    x1, x2 = xf[..., :half], xf[..., half:]
    return jnp.concatenate([x1 * cos - x2 * sin, x2 * cos + x1 * sin], axis=-1)


def _mlstm_chunk_step(carry, inp):
    c, n, m = carry
    q, k, v, ig, lf = inp
    L = q.shape[-2]
    causal = jnp.tril(jnp.ones((L, L), dtype=bool))
    b = jnp.cumsum(lf, axis=-1)
    log_d = jnp.where(causal, b[..., :, None] - b[..., None, :] + ig[..., None, :], -jnp.inf)
    log_inter = b + m[..., None]
    m_t = jnp.maximum(log_inter, jnp.max(log_d, axis=-1))
    d = jnp.exp(log_d - m_t[..., None])
    inter = jnp.exp(log_inter - m_t)
    s = jnp.einsum('bhtk,bhsk->bhts', q, k) * d
    num = inter[..., None] * jnp.einsum('bhtk,bhkv->bhtv', q, c) + jnp.einsum('bhts,bhsv->bhtv', s, v)
    den = inter * jnp.einsum('bhtk,bhk->bht', q, n) + jnp.sum(s, axis=-1)
    h = num / jnp.maximum(jnp.abs(den), jnp.exp(-m_t))[..., None]
    m_new = m_t[..., -1]
    w_src = jnp.exp(b[..., -1:] - b + ig - m_new[..., None])
    decay = jnp.exp(b[..., -1] + m - m_new)
    c_new = decay[..., None, None] * c + jnp.einsum('bhs,bhsk,bhsv->bhkv', w_src, k, v)
    n_new = decay[..., None] * n + jnp.einsum('bhs,bhsk->bhk', w_src, k)
    return (c_new, n_new, m_new), h


def mlstm_mixer(x, w_in, b_gates, g_head, w_out, c0, n0, m0):
    f32 = jnp.float32
    bsz, t, _ = x.shape
    proj = x @ w_in
    cuts = [A_QK_W, 2 * A_QK_W, 2 * A_QK_W + A_V_W, 2 * A_QK_W + 2 * A_V_W, 2 * A_QK_W + 2 * A_V_W + A_HEADS]
    q, k, v, o, ig, fg = jnp.split(proj, cuts, axis=-1)
    q = q.astype(f32).reshape(bsz, t, A_HEADS, A_DK) * (A_DK ** -0.5)
    k = k.astype(f32).reshape(bsz, t, A_HEADS, A_DK)
    v = v.astype(f32).reshape(bsz, t, A_HEADS, A_DV)
    ig = ig.astype(f32) + b_gates[:A_HEADS].astype(f32)
    lf = jax.nn.log_sigmoid(fg.astype(f32) + b_gates[A_HEADS:].astype(f32))
    chunk = A_CHUNK if t % A_CHUNK == 0 else t
    nc = t // chunk

    def to_chunks(a):
        a = a.reshape((bsz, nc, chunk) + a.shape[2:])
        return jnp.swapaxes(jnp.moveaxis(a, 1, 0), 2, 3)

    init = (c0.astype(f32), n0.astype(f32), m0.astype(f32))
    xs = (to_chunks(q), to_chunks(k), to_chunks(v), to_chunks(ig), to_chunks(lf))
    (c1, n1, m1), hs = lax.scan(_mlstm_chunk_step, init, xs)
    h = jnp.transpose(hs, (1, 0, 3, 2, 4)).reshape(bsz, t, A_HEADS, A_DV)
    h = h * lax.rsqrt(jnp.mean(h * h, axis=-1, keepdims=True) + EPS)
    h = h.reshape(bsz, t, A_V_W) * g_head.astype(f32) * jax.nn.sigmoid(o.astype(f32))
    y = h.astype(x.dtype) @ w_out
    return y, (c1, n1, m1)


def dilated_qkv(x, w_qkv, pos):
    bsz, t, _ = x.shape
    proj = (x @ w_qkv).reshape(bsz, t, 3, N_GROUPS_B, B_HEADS, B_DH)
    q = rotary(proj[:, :, 0], pos)
    k = rotary(proj[:, :, 1], pos)
    v = proj[:, :, 2].astype(jnp.float32)
    return q, k, v


def band_attention(q, k, v, dil, steps):
    bsz, t, h, dh = q.shape
    l = t // dil
    blk = steps
    nb = -(-l // blk)
    lp = nb * blk
    n = bsz * dil

    def residues(a):
        a = a.reshape(bsz, l, dil, h, dh).transpose(0, 2, 1, 3, 4).reshape(n, l, h, dh)
        return jnp.pad(a, ((0, 0), (0, lp - l), (0, 0), (0, 0)))

    def with_prev(a):
        prev = jnp.pad(a, ((0, 0), (blk, 0), (0, 0), (0, 0)))[:, :lp]
        return jnp.concatenate([prev.reshape(n, nb, blk, h, dh), a.reshape(n, nb, blk, h, dh)], axis=2)

    qb = residues(q).reshape(n, nb, blk, h, dh)
    kb = with_prev(residues(k))
    vb = with_prev(residues(v))
    qi = jnp.arange(blk)[:, None]
    kj = jnp.arange(2 * blk)[None, :]
    dist = qi - kj + blk
    key_pos = (jnp.arange(nb) * blk - blk)[:, None, None] + kj[None]
    valid = (dist >= 0) & (dist <= steps) & (key_pos >= 0)
    s = jnp.einsum('nbqhd,nbkhd->nbhqk', qb, kb) * (dh ** -0.5)
    s = jnp.where(valid[None, :, None], s, -jnp.inf)
    mx = jnp.max(s, axis=-1, keepdims=True)
    p = jnp.exp(s - mx)
    den = jnp.sum(p, axis=-1)
    o = jnp.einsum('nbhqk,nbkhd->nbqhd', p / den[..., None], vb)
    lse = mx[..., 0] + jnp.log(den)
    o = o.reshape(n, lp, h, dh)[:, :l].reshape(bsz, dil, l, h, dh).transpose(0, 2, 1, 3, 4).reshape(bsz, t, h, dh)
    lse = lse.transpose(0, 1, 3, 2).reshape(n, lp, h)[:, :l].reshape(bsz, dil, l, h).transpose(0, 2, 1, 3).reshape(bsz, t, h)
    return o, lse


def gather_attention(q, k_all, v_all, dil, steps, n_buf):
    bsz, s_len, h, dh = q.shape
    j = steps + 1
    idx = n_buf + jnp.arange(s_len)[:, None] - dil * jnp.arange(j)[None, :]
    valid = idx >= 0
    flat = jnp.maximum(idx, 0).reshape(-1)
    kg = jnp.take(k_all, flat, axis=1).reshape(bsz, s_len, j, h, dh)
    vg = jnp.take(v_all, flat, axis=1).reshape(bsz, s_len, j, h, dh)
    s = jnp.einsum('bshd,bsjhd->bshj', q, kg) * (dh ** -0.5)
    s = jnp.where(valid[None, :, None, :], s, -jnp.inf)
    mx = jnp.max(s, axis=-1, keepdims=True)
    p = jnp.exp(s - mx)
    den = jnp.sum(p, axis=-1)
    o = jnp.einsum('bshj,bsjhd->bshd', p / den[..., None], vg)
    lse = mx[..., 0] + jnp.log(den)
    return o, lse


def merge_groups(outs, lses, w_out, dtype):
    alpha = jax.nn.softmax(jnp.stack(lses, axis=0), axis=0)
    o = jnp.sum(alpha[..., None] * jnp.stack(outs, axis=0), axis=0)
    bsz, t = o.shape[:2]
    return o.reshape(bsz, t, B_HEADS * B_DH).astype(dtype) @ w_out


def dilated_mixer_prompt(x, w_qkv, w_out, pos):
    q, k, v = dilated_qkv(x, w_qkv, pos)
    t = x.shape[1]
    outs, lses, rows = [], [], []
    for g, (win, dil) in enumerate(B_GROUPS):
        o_g, l_g = band_attention(q[:, :, g], k[:, :, g], v[:, :, g], dil, win // dil)
        outs.append(o_g)
        lses.append(l_g)
        keep = min(win, t)
        rows.append(jnp.stack([k[:, t - keep:, g], v[:, t - keep:, g]], axis=2))
    return merge_groups(outs, lses, w_out, x.dtype), rows


def dilated_mixer_sample(x, w_qkv, w_out, pos, caches):
    q, k, v = dilated_qkv(x, w_qkv, pos)
    outs, lses, rows = [], [], []
    for g, (win, dil) in enumerate(B_GROUPS):
        cache = caches[g]
        n_buf = cache.shape[1]
        k_all = jnp.concatenate([cache[:, :, 0].astype(jnp.float32), k[:, :, g]], axis=1)
        v_all = jnp.concatenate([cache[:, :, 1].astype(jnp.float32), v[:, :, g]], axis=1)
        o_g, l_g = gather_attention(q[:, :, g], k_all, v_all, dil, win // dil, n_buf)
        outs.append(o_g)
        lses.append(l_g)
        rows.append(jnp.stack([k[:, :, g], v[:, :, g]], axis=2))
    return merge_groups(outs, lses, w_out, x.dtype), rows


def hier_moe(x, w_rc, w_rf, w_gate, w_up, w_down):
    shp = x.shape
    xf = x.reshape(-1, shp[-1])
    ntok = xf.shape[0]
    lc = (xf @ w_rc).astype(jnp.float32)
    pc = jax.nn.softmax(lc, axis=-1)
    _, grp = lax.top_k(lc, 1)
    p_grp = jnp.take_along_axis(pc, grp, axis=-1)[:, 0]
    lf = (xf @ w_rf).astype(jnp.float32).reshape(ntok, MOE_GROUPS, MOE_PER_GROUP)
    lf_sel = jnp.take_along_axis(lf, grp[:, :, None], axis=1)[:, 0]
    top_v, top_i = lax.top_k(lf_sel, MOE_TOPK)
    w_sel = jax.nn.softmax(top_v, axis=-1) * p_grp[:, None]
    e_idx = grp * MOE_PER_GROUP + top_i
    gate = jnp.sum(jax.nn.one_hot(e_idx, N_EXPERTS, dtype=jnp.float32) * w_sel[..., None], axis=1)
    h = jax.nn.silu(jnp.einsum('nd,edf->nef', xf, w_gate)) * jnp.einsum('nd,edf->nef', xf, w_up)
    h = h * gate[..., None].astype(h.dtype)
    y = jnp.einsum('nef,efd->nd', h, w_down)
    return y.reshape(shp)


def setup_inputs(seed: int = 0) -> dict:
    key = jax.random.key(seed)
    ks = jax.random.split(key, 32)
    f32 = jnp.float32

    def nrm(k, shape, scale):
        return jax.random.normal(k, shape, f32) * scale

    n_a, n_b = N_LAYERS_A, N_LAYERS_B
    hb = (B_HEADS, B_DH)
    return {
        'x_prompt': nrm(ks[0], (BATCH, SEQ, D_MODEL), 1.0),
        'x_sample': nrm(ks[1], (DEC_BATCH, DEC_SEQ, D_MODEL), 1.0),
        'state_mlstm_C': nrm(ks[2], (n_a, DEC_BATCH, A_HEADS, A_DK, A_DV), 1.0),
        'state_mlstm_n': nrm(ks[3], (n_a, DEC_BATCH, A_HEADS, A_DK), 1.0),
        'state_mlstm_m': nrm(ks[4], (n_a, DEC_BATCH, A_HEADS), 1.0),
        'cache_kv_w128': nrm(ks[5], (n_b, DEC_BATCH, min(B_GROUPS[0][0], PAST_LEN), 2) + hb, 1.0),
        'cache_kv_w512': nrm(ks[6], (n_b, DEC_BATCH, min(B_GROUPS[1][0], PAST_LEN), 2) + hb, 1.0),
        'cache_kv_w2048': nrm(ks[7], (n_b, DEC_BATCH, min(B_GROUPS[2][0], PAST_LEN), 2) + hb, 1.0),
        'g_norm_mix': 1.0 + nrm(ks[8], (DEPTH, D_MODEL), 0.05),
        'g_norm_ffn': 1.0 + nrm(ks[9], (DEPTH, D_MODEL), 0.05),
        'g_norm_final': 1.0 + nrm(ks[10], (D_MODEL,), 0.05),
        'w_in_a': nrm(ks[11], (n_a, D_MODEL, A_IN_W), D_MODEL ** -0.5),
        'b_gates_a': jnp.concatenate([nrm(ks[12], (n_a, A_HEADS), 0.1),
                                      3.0 + nrm(ks[13], (n_a, A_HEADS), 0.5)], axis=-1),
        'g_head_a': 1.0 + nrm(ks[14], (n_a, A_V_W), 0.05),
        'w_out_a': nrm(ks[15], (n_a, A_V_W, D_MODEL), A_V_W ** -0.5),
        'w_qkv_b': nrm(ks[16], (n_b, D_MODEL, B_QKV_W), D_MODEL ** -0.5),
        'w_out_b': nrm(ks[17], (n_b, B_HEADS * B_DH, D_MODEL), (B_HEADS * B_DH) ** -0.5),
        'w_router_coarse': nrm(ks[18], (DEPTH, D_MODEL, MOE_GROUPS), D_MODEL ** -0.5),
        'w_router_fine': nrm(ks[19], (DEPTH, D_MODEL, N_EXPERTS), D_MODEL ** -0.5),
        'w_gate_e': nrm(ks[20], (DEPTH, N_EXPERTS, D_MODEL, D_FF_EXPERT), D_MODEL ** -0.5),
        'w_up_e': nrm(ks[21], (DEPTH, N_EXPERTS, D_MODEL, D_FF_EXPERT), D_MODEL ** -0.5),
        'w_down_e': nrm(ks[22], (DEPTH, N_EXPERTS, D_FF_EXPERT, D_MODEL), D_FF_EXPERT ** -0.5),
    }


def reference(x_prompt, x_sample, state_mlstm_C, state_mlstm_n, state_mlstm_m,
              cache_kv_w128, cache_kv_w512, cache_kv_w2048,
              g_norm_mix, g_norm_ffn, g_norm_final,
              w_in_a, b_gates_a, g_head_a, w_out_a, w_qkv_b, w_out_b,
              w_router_coarse, w_router_fine, w_gate_e, w_up_e, w_down_e):
    f32 = jnp.float32
    bp, tp = x_prompt.shape[0], x_prompt.shape[1]
    pos_p = jnp.arange(tp, dtype=jnp.int32)
    pos_s = PAST_LEN + jnp.arange(x_sample.shape[1], dtype=jnp.int32)
    hp, hs = x_prompt, x_sample
    a_prompt, a_sample, b_prompt, b_sample = [], [], [], []
    for i in range(DEPTH):
        j = i // N_MIXERS
        np_in = rmsnorm(hp, g_norm_mix[i])
        ns_in = rmsnorm(hs, g_norm_mix[i])
        if i % N_MIXERS == 0:
            c0 = jnp.zeros((bp, A_HEADS, A_DK, A_DV), f32)
            n0 = jnp.zeros((bp, A_HEADS, A_DK), f32)
            m0 = jnp.zeros((bp, A_HEADS), f32)
            yp, st_p = mlstm_mixer(np_in, w_in_a[j], b_gates_a[j], g_head_a[j], w_out_a[j], c0, n0, m0)
            ys, st_s = mlstm_mixer(ns_in, w_in_a[j], b_gates_a[j], g_head_a[j], w_out_a[j],
                                   state_mlstm_C[j], state_mlstm_n[j], state_mlstm_m[j])
            a_prompt.append(st_p)
            a_sample.append(st_s)
        else:
            yp, rows_p = dilated_mixer_prompt(np_in, w_qkv_b[j], w_out_b[j], pos_p)
            ys, rows_s = dilated_mixer_sample(ns_in, w_qkv_b[j], w_out_b[j], pos_s,
                                              [cache_kv_w128[j], cache_kv_w512[j], cache_kv_w2048[j]])
            b_prompt.append(rows_p)
            b_sample.append(rows_s)
        hp = hp + yp
        hs = hs + ys
        hp = hp + hier_moe(rmsnorm(hp, g_norm_ffn[i]), w_router_coarse[i], w_router_fine[i],
                           w_gate_e[i], w_up_e[i], w_down_e[i])
        hs = hs + hier_moe(rmsnorm(hs, g_norm_ffn[i]), w_router_coarse[i], w_router_fine[i],
                           w_gate_e[i], w_up_e[i], w_down_e[i])
    y_prompt = rmsnorm(hp, g_norm_final)
    y_sample = rmsnorm(hs, g_norm_final)
    state_mlstm_C_prompt = jnp.stack([s[0] for s in a_prompt])
    state_mlstm_n_prompt = jnp.stack([s[1] for s in a_prompt])
    state_mlstm_m_prompt = jnp.stack([s[2] for s in a_prompt])
    state_mlstm_C_sample = jnp.stack([s[0] for s in a_sample])
    state_mlstm_n_sample = jnp.stack([s[1] for s in a_sample])
    state_mlstm_m_sample = jnp.stack([s[2] for s in a_sample])
    cache_kv_w128_prompt = jnp.stack([r[0] for r in b_prompt])
    cache_kv_w512_prompt = jnp.stack([r[1] for r in b_prompt])
    cache_kv_w2048_prompt = jnp.stack([r[2] for r in b_prompt])
    cache_kv_w128_sample = jnp.stack([r[0] for r in b_sample])
    cache_kv_w512_sample = jnp.stack([r[1] for r in b_sample])
    cache_kv_w2048_sample = jnp.stack([r[2] for r in b_sample])
    return (y_prompt, y_sample,
            state_mlstm_C_prompt, state_mlstm_n_prompt, state_mlstm_m_prompt,
            state_mlstm_C_sample, state_mlstm_n_sample, state_mlstm_m_sample,
            cache_kv_w128_prompt, cache_kv_w512_prompt, cache_kv_w2048_prompt,
            cache_kv_w128_sample, cache_kv_w512_sample, cache_kv_w2048_sample)
```

```python
import functools

import jax
import jax.numpy as jnp
from jax import lax
from jax.experimental import pallas as pl
from jax.experimental.pallas import tpu as pltpu

F32 = jnp.float32
BF16 = jnp.bfloat16

D_MODEL = 2048
PAST_LEN = 2048
EPS = 1e-6
ROPE_THETA = 10000.0

A_HEADS = 8
A_DV = 256
A_DK = 128
A_QK_W = A_HEADS * A_DK
A_V_W = A_HEADS * A_DV
A_MAIN_W = 2 * A_QK_W + 2 * A_V_W
A_CHUNK_PROMPT = 256

B_GROUPS = ((128, 1), (512, 4), (2048, 16))
B_HEADS = 16
B_DH = 128
B_HW = B_HEADS * B_DH
B_QKV_W = 9 * B_HW
B_STEPS = 128
B_BLK = 128

N_EXPERTS = 16
MOE_GROUPS = 4
MOE_PER_GROUP = 4
D_FF = 1024
FF_BLK = 512

LANE = 128
ROW_TILE = 512
MOE_TILE = 256
GATHER_TILE = 256
NEG = -1e30
VMEM_LIMIT = 52 * 1024 * 1024


def _cparams(*sem):
    return pltpu.CompilerParams(dimension_semantics=sem, vmem_limit_bytes=VMEM_LIMIT)


def _rmsnorm_body(x_ref, g_ref, o_ref):
    x = x_ref[...]
    y = x * lax.rsqrt(jnp.mean(x * x, axis=-1, keepdims=True) + EPS)
    o_ref[...] = (y * g_ref[...]).astype(o_ref.dtype)


def _rmsnorm(x, g, out_dtype, blk_off=0, n_blk=None):
    m, d = x.shape
    if n_blk is None:
        n_blk = m // ROW_TILE
    return pl.pallas_call(
        _rmsnorm_body,
        grid=(n_blk,),
        in_specs=[pl.BlockSpec((ROW_TILE, d), lambda i: (i + blk_off, 0)),
                  pl.BlockSpec((1, d), lambda i: (0, 0))],
        out_specs=pl.BlockSpec((ROW_TILE, d), lambda i: (i, 0)),
        out_shape=jax.ShapeDtypeStruct((n_blk * ROW_TILE, d), out_dtype),
        compiler_params=_cparams("parallel"),
        name="rmsnorm",
    )(x, g.reshape(1, d))


def _mm_body(*refs, tn, n_rot, has_res):
    it = iter(refs)
    x_ref = next(it)
    w_ref = next(it)
    cos_ref = next(it) if n_rot else None
    sin_ref = next(it) if n_rot else None
    res_ref = next(it) if has_res else None
    o_ref = next(it)
    acc = jnp.dot(x_ref[...], w_ref[...].astype(BF16), preferred_element_type=F32)
    if has_res:
        acc = acc + res_ref[...]
    if n_rot:
        j = pl.program_id(0)

        @pl.when(j < n_rot)
        def _():
            c = cos_ref[...]
            s = sin_ref[...]
            for h in range(tn // LANE):
                seg = acc[:, h * LANE:(h + 1) * LANE]
                rot = seg * c + pltpu.roll(seg, LANE // 2, 1) * s
                o_ref[:, h * LANE:(h + 1) * LANE] = rot.astype(o_ref.dtype)

        @pl.when(j >= n_rot)
        def _():
            o_ref[...] = acc.astype(o_ref.dtype)
    else:
        o_ref[...] = acc.astype(o_ref.dtype)


def _matmul(x, w, n_out, tn, out_dtype, res=None, rot=None):
    m, k = x.shape
    tm = ROW_TILE
    grid = (n_out // tn, m // tm)
    in_specs = [pl.BlockSpec((tm, k), lambda j, i: (i, 0)),
                pl.BlockSpec((k, tn), lambda j, i: (0, j))]
    args = [x, w]
    n_rot = 0
    if rot is not None:
        n_rot, cos, sin = rot
        in_specs += [pl.BlockSpec((tm, LANE), lambda j, i: (i, 0))] * 2
        args += [cos, sin]
    if res is not None:
        in_specs.append(pl.BlockSpec((tm, tn), lambda j, i: (i, j)))
        args.append(res)
    return pl.pallas_call(
        functools.partial(_mm_body, tn=tn, n_rot=n_rot, has_res=res is not None),
        grid=grid,
        in_specs=in_specs,
        out_specs=pl.BlockSpec((tm, tn), lambda j, i: (i, j)),
        out_shape=jax.ShapeDtypeStruct((m, n_out), out_dtype),
        compiler_params=_cparams("parallel", "parallel"),
        name="matmul",
    )(*args)


def _log_sigmoid(x):
    return jnp.minimum(x, 0.0) - jnp.log(1.0 + jnp.exp(-jnp.abs(x)))


def _mlstm_body(q_ref, k_ref, v_ref, o_ref, gc_ref, gr_ref, bc_ref, br_ref, gh_ref,
                c0_ref, n0_ref, m0_ref,
                h_ref, c1_ref, n1_ref, m1_ref,
                c_scr, n_scr, m_scr, *, L, t_valid):
    c_idx = pl.program_id(1)

    @pl.when(c_idx == 0)
    def _():
        c_scr[...] = c0_ref[0]
        n_scr[...] = n0_ref[0]
        m_scr[...] = m0_ref[0]

    scale = A_DK ** -0.5
    mm_dtype = BF16 if L >= 16 else F32
    gcol = gc_ref[0] + bc_ref[...]
    grow = gr_ref[0] + br_ref[:, :1]
    lane = lax.broadcasted_iota(jnp.int32, (L, LANE), 1)
    ti = lax.broadcasted_iota(jnp.int32, (L, L), 0)
    si = lax.broadcasted_iota(jnp.int32, (L, L), 1)
    causal = ti >= si
    pos0 = c_idx * L
    valid_c = (lax.broadcasted_iota(jnp.int32, (L, 1), 0) + pos0) < t_valid
    valid_r = (lax.broadcasted_iota(jnp.int32, (1, L), 1) + pos0) < t_valid

    for h in range(A_HEADS):
        q = q_ref[0, :, h * A_DK:(h + 1) * A_DK].astype(mm_dtype)
        k = k_ref[0, :, h * A_DK:(h + 1) * A_DK].astype(mm_dtype)
        v = v_ref[0, :, h * A_DV:(h + 1) * A_DV].astype(mm_dtype)
        ig_c = jnp.sum(jnp.where(lane == h, gcol, 0.0), axis=1, keepdims=True)
        fg_c = jnp.sum(jnp.where(lane == A_HEADS + h, gcol, 0.0), axis=1, keepdims=True)
        ig_r = grow[h:h + 1, :]
        fg_r = grow[A_HEADS + h:A_HEADS + h + 1, :]
        ig_c = jnp.where(valid_c, ig_c, NEG)
        ig_r = jnp.where(valid_r, ig_r, NEG)
        lf_c = jnp.where(valid_c, _log_sigmoid(fg_c), 0.0)
        lf_r = jnp.where(valid_r, _log_sigmoid(fg_r), 0.0)
        b_c = jnp.sum(jnp.where(causal, lf_r, 0.0), axis=1, keepdims=True)
        b_r = jnp.sum(jnp.where(ti <= si, lf_c, 0.0), axis=0, keepdims=True)
        m_prev = m_scr[h:h + 1, 0:1]
        c_prev = c_scr[h]
        n_prev = n_scr[h:h + 1, :]

        log_d = jnp.where(causal, b_c - b_r + ig_r, NEG)
        log_inter = b_c + m_prev
        m_t = jnp.maximum(log_inter, jnp.max(log_d, axis=1, keepdims=True))
        d = jnp.exp(log_d - m_t)
        inter = jnp.exp(log_inter - m_t)
        s = lax.dot_general(q, k, (((1,), (1,)), ((), ())), preferred_element_type=F32) * (scale * d)
        qc = jnp.dot(q, c_prev.astype(mm_dtype), preferred_element_type=F32) * scale
        sv = jnp.dot(s.astype(mm_dtype), v, preferred_element_type=F32)
        num = inter * qc + sv
        qn = jnp.sum(q.astype(F32) * n_prev, axis=1, keepdims=True) * scale
        den = inter * qn + jnp.sum(s, axis=1, keepdims=True)
        hh = num / jnp.maximum(jnp.abs(den), jnp.exp(-m_t))
        hh = hh * lax.rsqrt(jnp.mean(hh * hh, axis=1, keepdims=True) + EPS)
        og = o_ref[0, :, h * A_DV:(h + 1) * A_DV].astype(F32)
        hh = hh * gh_ref[:, h * A_DV:(h + 1) * A_DV] * (1.0 / (1.0 + jnp.exp(-og)))
        h_ref[0, :, h * A_DV:(h + 1) * A_DV] = hh.astype(h_ref.dtype)

        m_new = m_t[L - 1:L, :]
        b_last = b_c[L - 1:L, :]
        w_src = jnp.exp(b_last - b_c + ig_c - m_new)
        decay = jnp.exp(b_last + m_prev - m_new)
        kf = k.astype(F32)
        wv = w_src * v.astype(F32)
        kv = lax.dot_general(k, wv.astype(mm_dtype), (((0,), (0,)), ((), ())), preferred_element_type=F32)
        c_scr[h] = decay * c_prev + kv
        n_scr[h:h + 1, :] = decay * n_prev + jnp.sum(w_src * kf, axis=0, keepdims=True)
        m_scr[h:h + 1, :] = jnp.broadcast_to(m_new, (1, LANE))

    @pl.when(c_idx == pl.num_programs(1) - 1)
    def _():
        c1_ref[0] = c_scr[...]
        n1_ref[0] = n_scr[...]
        m1_ref[0] = m_scr[...]


def _mlstm(proj, gcol, grow, bcol, brow, g_head, c0, n0, m0, L, t_valid):
    bsz, t, _ = proj.shape
    nc = t // L
    state_specs = [pl.BlockSpec((1, A_HEADS, A_DK, A_DV), lambda b, c: (b, 0, 0, 0)),
                   pl.BlockSpec((1, A_HEADS, A_DK), lambda b, c: (b, 0, 0)),
                   pl.BlockSpec((1, A_HEADS, LANE), lambda b, c: (b, 0, 0))]
    return pl.pallas_call(
        functools.partial(_mlstm_body, L=L, t_valid=t_valid),
        grid=(bsz, nc),
        in_specs=[pl.BlockSpec((1, L, A_QK_W), lambda b, c: (b, c, 0)),
                  pl.BlockSpec((1, L, A_QK_W), lambda b, c: (b, c, 1)),
                  pl.BlockSpec((1, L, A_V_W), lambda b, c: (b, c, 1)),
                  pl.BlockSpec((1, L, A_V_W), lambda b, c: (b, c, 2)),
                  pl.BlockSpec((1, L, LANE), lambda b, c: (b, c, 0)),
                  pl.BlockSpec((1, 2 * A_HEADS, L), lambda b, c: (b, 0, c)),
                  pl.BlockSpec((1, LANE), lambda b, c: (0, 0)),
                  pl.BlockSpec((2 * A_HEADS, LANE), lambda b, c: (0, 0)),
                  pl.BlockSpec((1, A_V_W), lambda b, c: (0, 0))] + state_specs,
        out_specs=[pl.BlockSpec((1, L, A_V_W), lambda b, c: (b, c, 0))] + state_specs,
        out_shape=[jax.ShapeDtypeStruct((bsz, t, A_V_W), BF16),
                   jax.ShapeDtypeStruct((bsz, A_HEADS, A_DK, A_DV), F32),
                   jax.ShapeDtypeStruct((bsz, A_HEADS, A_DK), F32),
                   jax.ShapeDtypeStruct((bsz, A_HEADS, LANE), F32)],
        scratch_shapes=[pltpu.VMEM((A_HEADS, A_DK, A_DV), F32),
                        pltpu.VMEM((A_HEADS, A_DK), F32),
                        pltpu.VMEM((A_HEADS, LANE), F32)],
        compiler_params=_cparams("parallel", "arbitrary"),
        name="mlstm",
    )(proj, proj, proj, proj, gcol, grow, bcol, brow, g_head, c0, n0, m0)


def _pattn_body(q_ref, kp_ref, kc_ref, vp_ref, vc_ref, o_ref, lse_ref):
    qb = pl.program_id(2)
    scale = B_DH ** -0.5
    qi = lax.broadcasted_iota(jnp.int32, (B_BLK, B_BLK), 0)
    kj = lax.broadcasted_iota(jnp.int32, (B_BLK, B_BLK), 1)
    ok_cur = qi >= kj
    ok_prev = jnp.logical_and(kj >= qi, qb > 0)
    lane = lax.broadcasted_iota(jnp.int32, (B_BLK, LANE), 1)
    lse_all = jnp.zeros((B_BLK, LANE), F32)
    nt = (((1,), (1,)), ((), ()))
    for h in range(B_HEADS):
        sl = slice(h * B_DH, (h + 1) * B_DH)
        q = q_ref[:, sl].astype(BF16)
        s_c = lax.dot_general(q, kc_ref[:, sl].astype(BF16), nt, preferred_element_type=F32) * scale
        s_p = lax.dot_general(q, kp_ref[:, sl].astype(BF16), nt, preferred_element_type=F32) * scale
        s_c = jnp.where(ok_cur, s_c, NEG)
        s_p = jnp.where(ok_prev, s_p, NEG)
        mx = jnp.maximum(jnp.max(s_c, axis=1, keepdims=True), jnp.max(s_p, axis=1, keepdims=True))
        p_c = jnp.exp(s_c - mx)
        p_p = jnp.exp(s_p - mx)
        den = jnp.sum(p_c, axis=1, keepdims=True) + jnp.sum(p_p, axis=1, keepdims=True)
        o = (jnp.dot(p_c.astype(BF16), vc_ref[:, sl].astype(BF16), preferred_element_type=F32)
             + jnp.dot(p_p.astype(BF16), vp_ref[:, sl].astype(BF16), preferred_element_type=F32))
        o_ref[:, sl] = (o / den).astype(o_ref.dtype)
        lse_all = jnp.where(lane == h, mx + jnp.log(den), lse_all)
    lse_ref[0, 0] = lse_all


def _prompt_attention(qkv, bsz, t, g, dil):
    m = qkv.shape[0]
    l = t // dil
    nq = l // B_BLK
    view = qkv.reshape(m // dil, dil * B_QKV_W)
    ncol = B_QKV_W // B_HW

    def cur(which):
        return pl.BlockSpec((B_BLK, B_HW), lambda b, r, i: (b * nq + i, r * ncol + which * 3 + g))

    def prev(which):
        return pl.BlockSpec((B_BLK, B_HW),
                            lambda b, r, i: (b * nq + jnp.maximum(i - 1, 0), r * ncol + which * 3 + g))

    o, lse = pl.pallas_call(
        _pattn_body,
        grid=(bsz, dil, nq),
        in_specs=[cur(0), prev(1), cur(1), prev(2), cur(2)],
        out_specs=[pl.BlockSpec((B_BLK, B_HW), lambda b, r, i: (b * nq + i, r)),
                   pl.BlockSpec((1, 1, B_BLK, LANE), lambda b, r, i: (b, r, i, 0))],
        out_shape=[jax.ShapeDtypeStruct((bsz * l, dil * B_HW), F32),
                   jax.ShapeDtypeStruct((bsz, dil, l, LANE), F32)],
        compiler_params=_cparams("parallel", "parallel", "parallel"),
        name="prompt_attention",
    )(view, view, view, view, view)
    o = o.reshape(bsz * t, B_HW)
    lse = jnp.transpose(lse, (0, 2, 1, 3)).reshape(bsz * t, LANE)
    return o, lse


def _merge_body(o0_ref, o1_ref, o2_ref, l0_ref, l1_ref, l2_ref, e_ref, out_ref):
    l0, l1, l2 = l0_ref[...], l1_ref[...], l2_ref[...]
    mx = jnp.maximum(jnp.maximum(l0, l1), l2)
    a0, a1, a2 = jnp.exp(l0 - mx), jnp.exp(l1 - mx), jnp.exp(l2 - mx)
    inv = 1.0 / (a0 + a1 + a2)
    acc = jnp.zeros(out_ref.shape, F32)
    for a, o_ref in ((a0, o0_ref), (a1, o1_ref), (a2, o2_ref)):
        w = a * inv
        w_hi = w.astype(BF16)
        w_lo = (w - w_hi.astype(F32)).astype(BF16)
        wide = (jnp.dot(w_hi, e_ref[...], preferred_element_type=F32)
                + jnp.dot(w_lo, e_ref[...], preferred_element_type=F32))
        acc = acc + wide * o_ref[...]
    out_ref[...] = acc.astype(out_ref.dtype)


def _merge_groups(outs, lses, expand):
    m = outs[0].shape[0]
    tm = 256
    row = lambda i: (i, 0)
    return pl.pallas_call(
        _merge_body,
        grid=(m // tm,),
        in_specs=[pl.BlockSpec((tm, B_HW), row)] * 3 + [pl.BlockSpec((tm, LANE), row)] * 3
        + [pl.BlockSpec((LANE, B_HW), lambda i: (0, 0))],
        out_specs=pl.BlockSpec((tm, B_HW), row),
        out_shape=jax.ShapeDtypeStruct((m, B_HW), BF16),
        compiler_params=_cparams("parallel"),
        name="merge_groups",
    )(*outs, *lses, expand)


def _sattn_body(qkv_ref, c0_ref, c1_ref, c2_ref, seg_ref, exp_ref, o_ref):
    s_idx = pl.program_id(1)
    scale = B_DH ** -0.5
    new = qkv_ref[0]
    row8 = lax.broadcasted_iota(jnp.int32, (8, 1), 0)
    q_all = jnp.sum(jnp.where(row8 == s_idx, new[:, :3 * B_HW], 0.0), axis=0, keepdims=True)
    u = lax.broadcasted_iota(jnp.int32, (B_STEPS, 1), 0)
    caches = (c0_ref, c1_ref, c2_ref)
    scores = []
    for g in range(3):
        q = q_all[:, g * B_HW:(g + 1) * B_HW]
        k_old = caches[g][0, :, :B_HW]
        k_new = new[:, (3 + g) * B_HW:(4 + g) * B_HW]
        s_old = jnp.dot((k_old * q).astype(BF16), seg_ref[...], preferred_element_type=F32) * scale
        s_new = jnp.dot((k_new * q).astype(BF16), seg_ref[...], preferred_element_type=F32) * scale
        if g == 0:
            s_old = jnp.where(u >= s_idx, s_old, NEG)
            s_new = jnp.where(row8 <= s_idx, s_new, NEG)
        else:
            s_new = jnp.where(row8 == s_idx, s_new, NEG)
        scores.append((s_old, s_new))
    mx = None
    for s_old, s_new in scores:
        cand = jnp.maximum(jnp.max(s_old, axis=0, keepdims=True), jnp.max(s_new, axis=0, keepdims=True))
        mx = cand if mx is None else jnp.maximum(mx, cand)
    probs = []
    den = jnp.zeros((1, LANE), F32)
    for s_old, s_new in scores:
        p_old = jnp.exp(s_old - mx)
        p_new = jnp.exp(s_new - mx)
        den = den + jnp.sum(p_old, axis=0, keepdims=True) + jnp.sum(p_new, axis=0, keepdims=True)
        probs.append((p_old, p_new))
    inv = 1.0 / den
    acc = jnp.zeros((1, B_HW), F32)
    for g in range(3):
        p_old, p_new = probs[g]
        v_old = caches[g][0, :, B_HW:]
        v_new = new[:, (6 + g) * B_HW:(7 + g) * B_HW]
        w_old = jnp.dot((p_old * inv).astype(BF16), exp_ref[...], preferred_element_type=F32)
        w_new = jnp.dot((p_new * inv).astype(BF16), exp_ref[...], preferred_element_type=F32)
        acc = acc + jnp.sum(w_old * v_old, axis=0, keepdims=True) + jnp.sum(w_new * v_new, axis=0, keepdims=True)
    o_ref[0] = acc.astype(o_ref.dtype)


def _sample_attention(qkv_s, c0, c1, c2, seg, expand):
    bsz = qkv_s.shape[0]
    s_len = 4
    kvw = 2 * B_HW
    return pl.pallas_call(
        _sattn_body,
        grid=(bsz, s_len),
        in_specs=[pl.BlockSpec((1, 8, B_QKV_W), lambda b, s: (b, 0, 0)),
                  pl.BlockSpec((1, B_STEPS, kvw), lambda b, s: (b, 0, 0)),
                  pl.BlockSpec((1, B_STEPS, kvw), lambda b, s: (b, 0, s)),
                  pl.BlockSpec((1, B_STEPS, kvw), lambda b, s: (b, 0, s)),
                  pl.BlockSpec((B_HW, LANE), lambda b, s: (0, 0)),
                  pl.BlockSpec((LANE, B_HW), lambda b, s: (0, 0))],
        out_specs=pl.BlockSpec((1, 1, B_HW), lambda b, s: (b * s_len + s, 0, 0)),
        out_shape=jax.ShapeDtypeStruct((bsz * s_len, 1, B_HW), BF16),
        compiler_params=_cparams("parallel", "arbitrary"),
        name="sample_attention",
    )(qkv_s, c0, c1, c2, seg, expand)


def _gather_body(idx_ref, src_ref, *rest, k, tm, n_out, fold, has_base):
    if has_base:
        base_ref, o_ref, buf, sem = rest
    else:
        o_ref, buf, sem = rest
    i = pl.program_id(0)

    def copy(j, r):
        row = idx_ref[j * n_out + i * tm + r]
        return pltpu.make_async_copy(src_ref.at[pl.ds(row, 1), :], buf.at[j, pl.ds(r, 1), :], sem.at[j])

    def issue(r, carry):
        for j in range(k):
            copy(j, r).start()
        return carry

    def wait(r, carry):
        for j in range(k):
            copy(j, r).wait()
        return carry

    lax.fori_loop(0, tm, issue, 0)
    lax.fori_loop(0, tm, wait, 0)
    w = o_ref.shape[1]
    acc = base_ref[...] if has_base else jnp.zeros((tm, w), F32)
    for j in range(k):
        for f in range(fold):
            acc = acc + buf[j, :, f * w:(f + 1) * w]
    o_ref[...] = acc.astype(o_ref.dtype)


def _gather_rows(src, idx, out_dtype, fold=1, base=None):
    k, p = idx.shape
    tm = GATHER_TILE
    wsrc = src.shape[1]
    w = wsrc // fold
    in_specs = [pl.BlockSpec(memory_space=pl.ANY)]
    args = [src]
    if base is not None:
        in_specs.append(pl.BlockSpec((tm, w), lambda i, idx_ref: (i, 0)))
        args.append(base)
    return pl.pallas_call(
        functools.partial(_gather_body, k=k, tm=tm, n_out=p, fold=fold, has_base=base is not None),
        grid_spec=pltpu.PrefetchScalarGridSpec(
            num_scalar_prefetch=1,
            grid=(p // tm,),
            in_specs=in_specs,
            out_specs=pl.BlockSpec((tm, w), lambda i, idx_ref: (i, 0)),
            scratch_shapes=[pltpu.VMEM((k, tm, wsrc), F32), pltpu.SemaphoreType.DMA((k,))]),
        out_shape=jax.ShapeDtypeStruct((p, w), out_dtype),
        compiler_params=_cparams("arbitrary"),
        name="gather_rows",
    )(idx.reshape(k * p), *args)


def _experts_body(te_ref, nu_ref, x_ref, g_ref, wg_ref, wu_ref, wd_ref, o_ref):
    t = pl.program_id(1)

    @pl.when(t < nu_ref[0])
    def _():
        x = x_ref[...]
        a = jnp.dot(x, wg_ref[0, 0].astype(BF16), preferred_element_type=F32)
        b = jnp.dot(x, wu_ref[0, 0].astype(BF16), preferred_element_type=F32)
        hid = a * (1.0 / (1.0 + jnp.exp(-a))) * b * g_ref[...]
        o_ref[...] = jnp.dot(hid.astype(BF16), wd_ref[0, 0].astype(BF16), preferred_element_type=F32)

    @pl.when(t >= nu_ref[0])
    def _():
        o_ref[...] = jnp.zeros(o_ref.shape, F32)


def _experts(xs, row_gate, tile_expert, n_used, w_gate, w_up, w_down, layer):
    p, d = xs.shape
    tm = MOE_TILE
    n_tiles = p // tm
    n_fb = D_FF // FF_BLK
    return pl.pallas_call(
        _experts_body,
        grid_spec=pltpu.PrefetchScalarGridSpec(
            num_scalar_prefetch=2,
            grid=(n_fb, n_tiles),
            in_specs=[pl.BlockSpec((tm, d), lambda f, t, te, nu: (t, 0)),
                      pl.BlockSpec((tm, 1), lambda f, t, te, nu: (t, 0)),
                      pl.BlockSpec((1, 1, d, FF_BLK), lambda f, t, te, nu: (layer, te[t], 0, f)),
                      pl.BlockSpec((1, 1, d, FF_BLK), lambda f, t, te, nu: (layer, te[t], 0, f)),
                      pl.BlockSpec((1, 1, FF_BLK, d), lambda f, t, te, nu: (layer, te[t], f, 0))],
            out_specs=pl.BlockSpec((tm, d), lambda f, t, te, nu: (t, f))),
        out_shape=jax.ShapeDtypeStruct((p, n_fb * d), F32),
        compiler_params=_cparams("arbitrary", "arbitrary"),
        name="experts",
    )(tile_expert, n_used, xs, row_gate, w_gate, w_up, w_down)


def _route(logits):
    lc = logits[:, :MOE_GROUPS]
    pc = jax.nn.softmax(lc, axis=-1)
    _, grp = lax.top_k(lc, 1)
    p_grp = jnp.take_along_axis(pc, grp, axis=-1)[:, 0]
    lf = logits[:, MOE_GROUPS:MOE_GROUPS + N_EXPERTS].reshape(-1, MOE_GROUPS, MOE_PER_GROUP)
    lf_sel = jnp.take_along_axis(lf, grp[:, :, None], axis=1)[:, 0]
    top_v, top_i = lax.top_k(lf_sel, 2)
    w_sel = jax.nn.softmax(top_v, axis=-1) * p_grp[:, None]
    e_idx = grp * MOE_PER_GROUP + top_i
    return e_idx.astype(jnp.int32), w_sel


def _dispatch_plan(e_idx, w_sel, n_rows):
    tm = MOE_TILE
    n_pair = e_idx.shape[0] * 2
    flat_e = e_idx.reshape(-1)
    onehot = (flat_e[:, None] == jnp.arange(N_EXPERTS, dtype=jnp.int32)[None, :]).astype(jnp.int32)
    incl = jnp.cumsum(onehot, axis=0)
    rank = jnp.sum((incl - onehot) * onehot, axis=1)
    counts = incl[-1]
    padded = ((counts + tm - 1) // tm) * tm
    ends = jnp.cumsum(padded)
    starts = ends - padded
    pos = starts[flat_e] + rank
    row_token = jnp.zeros((n_rows,), jnp.int32).at[pos].set(jnp.arange(n_pair, dtype=jnp.int32) // 2)
    row_gate = jnp.zeros((n_rows,), F32).at[pos].set(w_sel.reshape(-1))
    n_tiles = n_rows // tm
    tile_start = jnp.arange(n_tiles, dtype=jnp.int32) * tm
    tile_expert = jnp.sum((tile_start[:, None] >= ends[None, :]).astype(jnp.int32), axis=1)
    n_used = (ends[-1] // tm).astype(jnp.int32)
    last_e = jnp.max(jnp.where(counts > 0, jnp.arange(N_EXPERTS, dtype=jnp.int32), 0))
    tile_expert = jnp.minimum(tile_expert, last_e).astype(jnp.int32)
    return pos.reshape(-1, 2), row_token, row_gate, tile_expert, n_used.reshape(1)


def _moe(x, g_norm, w_router, w_gate, w_up, w_down, layer):
    n, d = x.shape
    xn = _rmsnorm(x, g_norm, F32)
    logits = _matmul(xn.astype(BF16), w_router, LANE, LANE, F32)
    e_idx, w_sel = _route(logits)
    n_rows = ((2 * n + N_EXPERTS * (MOE_TILE - 1)) // MOE_TILE + 1) * MOE_TILE
    pos, row_token, row_gate, tile_expert, n_used = _dispatch_plan(e_idx, w_sel, n_rows)
    xs = _gather_rows(xn, row_token.reshape(1, n_rows), BF16)
    ys = _experts(xs, row_gate.reshape(n_rows, 1), tile_expert, n_used, w_gate, w_up, w_down, layer)
    return _gather_rows(ys, jnp.transpose(pos), F32, fold=D_FF // FF_BLK, base=x)


def _rope_tables(bp, tp, bs, ts):
    half = B_DH // 2
    inv_freq = ROPE_THETA ** (-jnp.arange(half, dtype=F32) / half)
    pos = jnp.concatenate([jnp.tile(jnp.arange(tp, dtype=jnp.int32), bp),
                           jnp.tile(PAST_LEN + jnp.arange(ts, dtype=jnp.int32), bs)])
    ang = pos.astype(F32)[:, None] * inv_freq[None, :]
    cos = jnp.cos(ang)
    sin = jnp.sin(ang)
    return jnp.concatenate([cos, cos], axis=1), jnp.concatenate([-sin, sin], axis=1)


def kernel(x_prompt, x_sample, state_mlstm_C, state_mlstm_n, state_mlstm_m, cache_kv_w128, cache_kv_w512, cache_kv_w2048, g_norm_mix, g_norm_ffn, g_norm_final, w_in_a, b_gates_a, g_head_a, w_out_a, w_qkv_b, w_out_b, w_router_coarse, w_router_fine, w_gate_e, w_up_e, w_down_e):
    bp, tp, d = x_prompt.shape
    bs, ts, _ = x_sample.shape
    n_p, n_s = bp * tp, bs * ts
    x = jnp.concatenate([x_prompt.reshape(n_p, d), x_sample.reshape(n_s, d)], axis=0)

    def router_w(i):
        w = jnp.concatenate([w_router_coarse[i], w_router_fine[i]], axis=1)
        return jnp.pad(w, ((0, 0), (0, LANE - w.shape[1])))

    xn = _rmsnorm(x, g_norm_mix[0], BF16)
    w_in = w_in_a[0]
    proj = _matmul(xn, w_in, A_MAIN_W, 1024, BF16)
    w_g = jnp.pad(w_in[:, A_MAIN_W:], ((0, 0), (0, LANE - 2 * A_HEADS)))
    gates = _matmul(xn, w_g, LANE, LANE, F32)
    bias = jnp.pad(b_gates_a[0], (0, LANE - 2 * A_HEADS))
    bcol = bias.reshape(1, LANE)
    brow = jnp.broadcast_to(b_gates_a[0][:, None], (2 * A_HEADS, LANE))
    g_head = g_head_a[0].reshape(1, A_V_W)

    def gate_views(gt, bsz, t):
        gc = gt.reshape(bsz, t, LANE)
        return gc, jnp.transpose(gc[:, :, :2 * A_HEADS], (0, 2, 1))

    gc_p, gr_p = gate_views(gates[:n_p], bp, tp)
    zeros_c = jnp.zeros((bp, A_HEADS, A_DK, A_DV), F32)
    zeros_n = jnp.zeros((bp, A_HEADS, A_DK), F32)
    zeros_m = jnp.zeros((bp, A_HEADS, LANE), F32)
    h_p, c_p, nn_p, m_p = _mlstm(proj[:n_p].reshape(bp, tp, A_MAIN_W), gc_p, gr_p, bcol, brow, g_head,
                                 zeros_c, zeros_n, zeros_m, A_CHUNK_PROMPT, tp)
    ts_pad = 8
    pad_t = lambda a: jnp.pad(a, ((0, 0), (0, ts_pad - ts), (0, 0)))
    gc_s, gr_s = gate_views(pad_t(gates[n_p:].reshape(bs, ts, LANE)), bs, ts_pad)
    m0_s = jnp.broadcast_to(state_mlstm_m[0][:, :, None], (bs, A_HEADS, LANE))
    h_s, c_s, nn_s, m_s = _mlstm(pad_t(proj[n_p:].reshape(bs, ts, A_MAIN_W)), gc_s, gr_s, bcol, brow, g_head,
                                 state_mlstm_C[0], state_mlstm_n[0], m0_s, ts_pad, ts)
    h_all = jnp.concatenate([h_p.reshape(n_p, A_V_W), h_s[:, :ts].reshape(n_s, A_V_W)], axis=0)
    x = _matmul(h_all, w_out_a[0], d, 1024, F32, res=x)
    x = _moe(x, g_norm_ffn[0], router_w(0), w_gate_e, w_up_e, w_down_e, 0)

    xn = _rmsnorm(x, g_norm_mix[1], BF16)
    cos, sin = _rope_tables(bp, tp, bs, ts)
    qkv = _matmul(xn, w_qkv_b[0], B_QKV_W, 1024, F32, rot=(2 * 3 * B_HW // 1024, cos, sin))
    eye = (jnp.arange(B_HW, dtype=jnp.int32)[:, None] // B_DH
           == jnp.arange(LANE, dtype=jnp.int32)[None, :]).astype(BF16)
    expand = jnp.transpose(eye)
    outs, lses = [], []
    for g, (win, dil) in enumerate(B_GROUPS):
        o_g, l_g = _prompt_attention(qkv, bp, tp, g, dil)
        outs.append(o_g)
        lses.append(l_g)
    attn_p = _merge_groups(outs, lses, expand)
    qkv_s = jnp.pad(qkv[n_p:].reshape(bs, ts, B_QKV_W), ((0, 0), (0, 8 - ts), (0, 0)))
    kvw = 2 * B_HW
    attn_s = _sample_attention(qkv_s,
                               cache_kv_w128[0].reshape(bs, B_STEPS, kvw),
                               cache_kv_w512[0].reshape(bs, B_STEPS, 4 * kvw),
                               cache_kv_w2048[0].reshape(bs, B_STEPS, 16 * kvw),
                               eye, expand)
    attn = jnp.concatenate([attn_p, attn_s.reshape(n_s, B_HW)], axis=0)
    x = _matmul(attn, w_out_b[0], d, 1024, F32, res=x)
    x = _moe(x, g_norm_ffn[1], router_w(1), w_gate_e, w_up_e, w_down_e, 1)

    y_prompt = _rmsnorm(x, g_norm_final, F32, 0, n_p // ROW_TILE).reshape(bp, tp, d)
    y_sample = _rmsnorm(x, g_norm_final, F32, n_p // ROW_TILE, n_s // ROW_TILE).reshape(bs, ts, d)

    kv = qkv[:, 3 * B_HW:].reshape(n_p + n_s, 2, 3, B_HEADS, B_DH)
    kv_p = kv[:n_p].reshape(bp, tp, 2, 3, B_HEADS, B_DH)
    kv_s = kv[n_p:].reshape(bs, ts, 2, 3, B_HEADS, B_DH)
    caches_p = [kv_p[:, tp - min(win, tp):, :, g][None] for g, (win, _) in enumerate(B_GROUPS)]
    caches_s = [kv_s[:, :, :, g][None] for g in range(3)]
    return (y_prompt, y_sample,
            c_p[None], nn_p[None], m_p[:, :, 0][None],
            c_s[None], nn_s[None], m_s[:, :, 0][None],
            caches_p[0], caches_p[1], caches_p[2],
            caches_s[0], caches_s[1], caches_s[2])
```

```python
import functools

import jax
import jax.numpy as jnp
from jax import lax
from jax.experimental import pallas as pl
from jax.experimental.pallas import tpu as pltpu

F32 = jnp.float32
BF16 = jnp.bfloat16

D_MODEL = 2048
PAST_LEN = 2048
EPS = 1e-6
ROPE_THETA = 10000.0

A_HEADS = 8
A_DV = 256
A_DK = 128
A_QK_W = A_HEADS * A_DK
A_V_W = A_HEADS * A_DV
A_MAIN_W = 2 * A_QK_W + 2 * A_V_W
A_CHUNK_PROMPT = 256

B_GROUPS = ((128, 1), (512, 4), (2048, 16))
B_HEADS = 16
B_DH = 128
B_HW = B_HEADS * B_DH
B_QKV_W = 9 * B_HW
B_STEPS = 128
PA_BLK = 256
PA_PREV = tuple(-(-win // PA_BLK) for win, _ in B_GROUPS)
PA_NTILE = (2, 3, 2)
PA_TILE0 = (0, 2, 5)

N_EXPERTS = 16
MOE_GROUPS = 4
MOE_PER_GROUP = 4
D_FF = 1024
FF_BLK = 512

LANE = 128
ROW_TILE = 512
MOE_TILE = 256
GATHER_TILE = 256
NEG = -1e30
VMEM_LIMIT = 52 * 1024 * 1024


def _cparams(*sem):
    return pltpu.CompilerParams(dimension_semantics=sem, vmem_limit_bytes=VMEM_LIMIT)


def _rms(x, g):
    return x * lax.rsqrt(jnp.mean(x * x, axis=-1, keepdims=True) + EPS) * g


def _rmsnorm_body(x_ref, g_ref, o_ref):
    o_ref[...] = _rms(x_ref[...], g_ref[...]).astype(o_ref.dtype)


def _rmsnorm(x, g, out_dtype, blk_off=0, n_blk=None):
    m, d = x.shape
    if n_blk is None:
        n_blk = m // ROW_TILE
    return pl.pallas_call(
        _rmsnorm_body,
        grid=(n_blk,),
        in_specs=[pl.BlockSpec((ROW_TILE, d), lambda i: (i + blk_off, 0)),
                  pl.BlockSpec((1, d), lambda i: (0, 0))],
        out_specs=pl.BlockSpec((ROW_TILE, d), lambda i: (i, 0)),
        out_shape=jax.ShapeDtypeStruct((n_blk * ROW_TILE, d), out_dtype),
        compiler_params=_cparams("parallel"),
        name="rmsnorm",
    )(x, g.reshape(1, d))


def _norm_route_body(x_ref, g_ref, wr_ref, xn_ref, route_ref):
    y = _rms(x_ref[...], g_ref[...])
    xn_ref[...] = y
    logits = jnp.dot(y.astype(BF16), wr_ref[...].astype(BF16), preferred_element_type=F32)
    lane = lax.broadcasted_iota(jnp.int32, logits.shape, 1)
    lane_f = lane.astype(F32)
    far = float(LANE)

    def first_max(vals):
        top = jnp.max(vals, axis=1, keepdims=True)
        return top, jnp.min(jnp.where(vals == top, lane_f, far), axis=1, keepdims=True)

    is_coarse = lane < MOE_GROUPS
    mc, grp = first_max(jnp.where(is_coarse, logits, NEG))
    p_grp = 1.0 / jnp.sum(jnp.where(is_coarse, jnp.exp(logits - mc), 0.0), axis=1, keepdims=True)
    lo = MOE_GROUPS + MOE_PER_GROUP * grp
    in_grp = jnp.logical_and(lane_f >= lo, lane_f < lo + MOE_PER_GROUP)
    fine = jnp.where(in_grp, logits, NEG)
    t1, i1 = first_max(fine)
    t2, i2 = first_max(jnp.where(lane_f == i1, NEG, fine))
    e = jnp.exp(t2 - t1)
    w1 = p_grp / (1.0 + e)
    w2 = w1 * e
    route = jnp.where(lane == 0, i1 - MOE_GROUPS,
                      jnp.where(lane == 1, i2 - MOE_GROUPS,
                                jnp.where(lane == 2, w1, jnp.where(lane == 3, w2, 0.0))))
    route_ref[...] = route


def _norm_route(x, g, w_router):
    m, d = x.shape
    return pl.pallas_call(
        _norm_route_body,
        grid=(m // ROW_TILE,),
        in_specs=[pl.BlockSpec((ROW_TILE, d), lambda i: (i, 0)),
                  pl.BlockSpec((1, d), lambda i: (0, 0)),
                  pl.BlockSpec((d, LANE), lambda i: (0, 0))],
        out_specs=[pl.BlockSpec((ROW_TILE, d), lambda i: (i, 0)),
                   pl.BlockSpec((ROW_TILE, LANE), lambda i: (i, 0))],
        out_shape=[jax.ShapeDtypeStruct((m, d), F32), jax.ShapeDtypeStruct((m, LANE), F32)],
        compiler_params=_cparams("parallel"),
        name="norm_route",
    )(x, g.reshape(1, d), w_router)


def _mm_body(*refs, tn, n_rot, has_res):
    it = iter(refs)
    x_ref = next(it)
    w_ref = next(it)
    cos_ref = next(it) if n_rot else None
    sin_ref = next(it) if n_rot else None
    res_ref = next(it) if has_res else None
    o_ref = next(it)
    acc = jnp.dot(x_ref[...], w_ref[...].astype(BF16), preferred_element_type=F32)
    if has_res:
        acc = acc + res_ref[...]
    if n_rot:
        j = pl.program_id(0)

        @pl.when(j < n_rot)
        def _():
            c = cos_ref[...]
            s = sin_ref[...]
            for h in range(tn // LANE):
                seg = acc[:, h * LANE:(h + 1) * LANE]
                rot = seg * c + pltpu.roll(seg, LANE // 2, 1) * s
                o_ref[h] = rot.astype(o_ref.dtype)

        @pl.when(j >= n_rot)
        def _():
            for h in range(tn // LANE):
                o_ref[h] = acc[:, h * LANE:(h + 1) * LANE].astype(o_ref.dtype)
    else:
        o_ref[...] = acc.astype(o_ref.dtype)


def _matmul(x, w, n_out, tn, out_dtype, res=None, rot=None):
    m, k = x.shape
    tm = ROW_TILE
    grid = (n_out // tn, m // tm)
    in_specs = [pl.BlockSpec((tm, k), lambda j, i: (i, 0)),
                pl.BlockSpec((k, tn), lambda j, i: (0, j))]
    args = [x, w]
    n_rot = 0
    if rot is not None:
        n_rot, cos, sin = rot
        in_specs += [pl.BlockSpec((tm, LANE), lambda j, i: (i, 0))] * 2
        args += [cos, sin]
    if res is not None:
        in_specs.append(pl.BlockSpec((tm, tn), lambda j, i: (i, j)))
        args.append(res)
    if rot is not None:
        out_specs = pl.BlockSpec((tn // LANE, tm, LANE), lambda j, i: (j, i, 0))
        out_shape = jax.ShapeDtypeStruct((n_out // LANE, m, LANE), out_dtype)
    else:
        out_specs = pl.BlockSpec((tm, tn), lambda j, i: (i, j))
        out_shape = jax.ShapeDtypeStruct((m, n_out), out_dtype)
    return pl.pallas_call(
        functools.partial(_mm_body, tn=tn, n_rot=n_rot, has_res=res is not None),
        grid=grid,
        in_specs=in_specs,
        out_specs=out_specs,
        out_shape=out_shape,
        compiler_params=_cparams("parallel", "parallel"),
        name="matmul",
    )(*args)


def _log_sigmoid(x):
    return jnp.minimum(x, 0.0) - jnp.log(1.0 + jnp.exp(-jnp.abs(x)))


def _mlstm_body(q_ref, k_ref, v_ref, o_ref, gc_ref, gr_ref, bc_ref, br_ref, gh_ref,
                c0_ref, n0_ref, m0_ref,
                h_ref, c1_ref, n1_ref, m1_ref,
                c_scr, n_scr, m_scr, *, L, t_valid):
    c_idx = pl.program_id(1)

    @pl.when(c_idx == 0)
    def _():
        c_scr[...] = c0_ref[0]
        n_scr[...] = n0_ref[0]
        m_scr[...] = m0_ref[0]

    scale = A_DK ** -0.5
    mm_dtype = BF16 if L >= 16 else F32
    gcol = gc_ref[0] + bc_ref[...]
    grow = gr_ref[0] + br_ref[:, :1]
    lane = lax.broadcasted_iota(jnp.int32, (L, LANE), 1)
    ti = lax.broadcasted_iota(jnp.int32, (L, L), 0)
    si = lax.broadcasted_iota(jnp.int32, (L, L), 1)
    causal = ti >= si
    pos0 = c_idx * L
    valid_c = (lax.broadcasted_iota(jnp.int32, (L, 1), 0) + pos0) < t_valid
    valid_r = (lax.broadcasted_iota(jnp.int32, (1, L), 1) + pos0) < t_valid

    for h in range(A_HEADS):
        q = q_ref[0, :, h * A_DK:(h + 1) * A_DK].astype(mm_dtype)
        k = k_ref[0, :, h * A_DK:(h + 1) * A_DK].astype(mm_dtype)
        v = v_ref[0, :, h * A_DV:(h + 1) * A_DV].astype(mm_dtype)
        ig_c = jnp.sum(jnp.where(lane == h, gcol, 0.0), axis=1, keepdims=True)
        fg_c = jnp.sum(jnp.where(lane == A_HEADS + h, gcol, 0.0), axis=1, keepdims=True)
        ig_r = grow[h:h + 1, :]
        fg_r = grow[A_HEADS + h:A_HEADS + h + 1, :]
        ig_c = jnp.where(valid_c, ig_c, NEG)
        ig_r = jnp.where(valid_r, ig_r, NEG)
        lf_c = jnp.where(valid_c, _log_sigmoid(fg_c), 0.0)
        lf_r = jnp.where(valid_r, _log_sigmoid(fg_r), 0.0)
        b_c = jnp.sum(jnp.where(causal, lf_r, 0.0), axis=1, keepdims=True)
        b_r = jnp.sum(jnp.where(ti <= si, lf_c, 0.0), axis=0, keepdims=True)
        m_prev = m_scr[h:h + 1, 0:1]
        c_prev = c_scr[h]
        n_prev = n_scr[h:h + 1, :]

        log_d = jnp.where(causal, b_c - b_r + ig_r, NEG)
        log_inter = b_c + m_prev
        m_t = jnp.maximum(log_inter, jnp.max(log_d, axis=1, keepdims=True))
        d = jnp.exp(log_d - m_t)
        inter = jnp.exp(log_inter - m_t)
        s = lax.dot_general(q, k, (((1,), (1,)), ((), ())), preferred_element_type=F32) * (scale * d)
        qc = jnp.dot(q, c_prev.astype(mm_dtype), preferred_element_type=F32) * scale
        sv = jnp.dot(s.astype(mm_dtype), v, preferred_element_type=F32)
        num = inter * qc + sv
        qn = jnp.sum(q.astype(F32) * n_prev, axis=1, keepdims=True) * scale
        den = inter * qn + jnp.sum(s, axis=1, keepdims=True)
        hh = num / jnp.maximum(jnp.abs(den), jnp.exp(-m_t))
        hh = hh * lax.rsqrt(jnp.mean(hh * hh, axis=1, keepdims=True) + EPS)
        og = o_ref[0, :, h * A_DV:(h + 1) * A_DV].astype(F32)
        hh = hh * gh_ref[:, h * A_DV:(h + 1) * A_DV] * (1.0 / (1.0 + jnp.exp(-og)))
        h_ref[0, :, h * A_DV:(h + 1) * A_DV] = hh.astype(h_ref.dtype)

        m_new = m_t[L - 1:L, :]
        b_last = b_c[L - 1:L, :]
        w_src = jnp.exp(b_last - b_c + ig_c - m_new)
        decay = jnp.exp(b_last + m_prev - m_new)
        kf = k.astype(F32)
        wv = w_src * v.astype(F32)
        kv = lax.dot_general(k, wv.astype(mm_dtype), (((0,), (0,)), ((), ())), preferred_element_type=F32)
        c_scr[h] = decay * c_prev + kv
        n_scr[h:h + 1, :] = decay * n_prev + jnp.sum(w_src * kf, axis=0, keepdims=True)
        m_scr[h:h + 1, :] = jnp.broadcast_to(m_new, (1, LANE))

    @pl.when(c_idx == pl.num_programs(1) - 1)
    def _():
        c1_ref[0] = c_scr[...]
        n1_ref[0] = n_scr[...]
        m1_ref[0] = m_scr[...]


def _mlstm(proj, gcol, grow, bcol, brow, g_head, c0, n0, m0, L, t_valid):
    bsz, t, _ = proj.shape
    nc = t // L
    state_specs = [pl.BlockSpec((1, A_HEADS, A_DK, A_DV), lambda b, c: (b, 0, 0, 0)),
                   pl.BlockSpec((1, A_HEADS, A_DK), lambda b, c: (b, 0, 0)),
                   pl.BlockSpec((1, A_HEADS, LANE), lambda b, c: (b, 0, 0))]
    return pl.pallas_call(
        functools.partial(_mlstm_body, L=L, t_valid=t_valid),
        grid=(bsz, nc),
        in_specs=[pl.BlockSpec((1, L, A_QK_W), lambda b, c: (b, c, 0)),
                  pl.BlockSpec((1, L, A_QK_W), lambda b, c: (b, c, 1)),
                  pl.BlockSpec((1, L, A_V_W), lambda b, c: (b, c, 1)),
                  pl.BlockSpec((1, L, A_V_W), lambda b, c: (b, c, 2)),
                  pl.BlockSpec((1, L, LANE), lambda b, c: (b, c, 0)),
                  pl.BlockSpec((1, 2 * A_HEADS, L), lambda b, c: (b, 0, c)),
                  pl.BlockSpec((1, LANE), lambda b, c: (0, 0)),
                  pl.BlockSpec((2 * A_HEADS, LANE), lambda b, c: (0, 0)),
                  pl.BlockSpec((1, A_V_W), lambda b, c: (0, 0))] + state_specs,
        out_specs=[pl.BlockSpec((1, L, A_V_W), lambda b, c: (b, c, 0))] + state_specs,
        out_shape=[jax.ShapeDtypeStruct((bsz, t, A_V_W), BF16),
                   jax.ShapeDtypeStruct((bsz, A_HEADS, A_DK, A_DV), F32),
                   jax.ShapeDtypeStruct((bsz, A_HEADS, A_DK), F32),
                   jax.ShapeDtypeStruct((bsz, A_HEADS, LANE), F32)],
        scratch_shapes=[pltpu.VMEM((A_HEADS, A_DK, A_DV), F32),
                        pltpu.VMEM((A_HEADS, A_DK), F32),
                        pltpu.VMEM((A_HEADS, LANE), F32)],
        compiler_params=_cparams("parallel", "arbitrary"),
        name="mlstm",
    )(proj, proj, proj, proj, gcol, grow, bcol, brow, g_head, c0, n0, m0)


def _band_bias_tiles():
    qi = jnp.arange(PA_BLK, dtype=jnp.int32)[:, None]
    kj = jnp.arange(PA_BLK, dtype=jnp.int32)[None, :]
    tiles = []
    for (win, dil), n in zip(B_GROUPS, PA_NTILE):
        for off in range(n):
            dist = off * PA_BLK + qi - kj
            ok = (dist >= 0) & (dist <= win) & (jnp.mod(dist, dil) == 0)
            tiles.append(jnp.where(ok, 0.0, NEG).astype(F32))
    return jnp.stack(tiles)


def _pattn_body(q0_ref, q1_ref, q2_ref, k0_ref, k1_ref, k2_ref, v0_ref, v1_ref, v2_ref,
                bias_ref, o_ref, kv_scr, s_scr, *, t):
    q_refs = (q0_ref, q1_ref, q2_ref)
    for i, r in enumerate((k0_ref, k1_ref, k2_ref, v0_ref, v1_ref, v2_ref)):
        kv_scr[i] = r[0].astype(BF16)
    scale = B_DH ** -0.5
    nt = (((1,), (1,)), ((), ()))
    for qb in range(t // PA_BLK):
        rows = slice(qb * PA_BLK, (qb + 1) * PA_BLK)
        chunks = []
        for g in range(len(B_GROUPS)):
            for off in range(min(PA_PREV[g], qb) + 1):
                chunks.append((g, qb - off, PA_TILE0[g] + min(off, PA_NTILE[g] - 1)))
        qs = [r[0, rows, :].astype(BF16) for r in q_refs]
        mrun = None
        for c, (g, kb, tile) in enumerate(chunks):
            k = kv_scr[g, kb * PA_BLK:(kb + 1) * PA_BLK, :]
            s = lax.dot_general(qs[g], k, nt, preferred_element_type=F32) * scale + bias_ref[tile]
            s_scr[:, c * PA_BLK:(c + 1) * PA_BLK] = s
            mrun = s if mrun is None else jnp.maximum(mrun, s)
        mx = jnp.max(mrun, axis=1, keepdims=True)
        lrun = None
        acc = jnp.zeros((PA_BLK, B_DH), F32)
        for c, (g, kb, tile) in enumerate(chunks):
            p = jnp.exp(s_scr[:, c * PA_BLK:(c + 1) * PA_BLK] - mx)
            lrun = p if lrun is None else lrun + p
            v = kv_scr[len(B_GROUPS) + g, kb * PA_BLK:(kb + 1) * PA_BLK, :]
            acc = acc + jnp.dot(p.astype(BF16), v, preferred_element_type=F32)
        den = jnp.sum(lrun, axis=1, keepdims=True)
        o_ref[rows, :] = (acc / den).astype(o_ref.dtype)


def _prompt_attention(qkvh, bias, bsz, t):
    ng = len(B_GROUPS)
    assert t % PA_BLK == 0 and t <= B_GROUPS[-1][0]

    def spec(which, g):
        return pl.BlockSpec((1, t, B_DH), lambda b, h: ((which * ng + g) * B_HEADS + h, b, 0))

    n_chunk = sum(min(p, t // PA_BLK - 1) + 1 for p in PA_PREV)
    return pl.pallas_call(
        functools.partial(_pattn_body, t=t),
        grid=(bsz, B_HEADS),
        in_specs=[spec(w, g) for w in range(3) for g in range(ng)]
        + [pl.BlockSpec((sum(PA_NTILE), PA_BLK, PA_BLK), lambda b, h: (0, 0, 0))],
        out_specs=pl.BlockSpec((t, B_DH), lambda b, h: (b, h)),
        out_shape=jax.ShapeDtypeStruct((bsz * t, B_HW), BF16),
        scratch_shapes=[pltpu.VMEM((2 * ng, t, B_DH), BF16),
                        pltpu.VMEM((PA_BLK, n_chunk * PA_BLK), F32)],
        compiler_params=_cparams("parallel", "parallel"),
        name="prompt_attention",
    )(*([qkvh] * 9), bias)


def _sattn_body(qkv_ref, c0_ref, c1_ref, c2_ref, o_ref, *, s_len):
    s_idx = pl.program_id(1)
    scale = B_DH ** -0.5
    ng = len(B_GROUPS)
    caches = (c0_ref, c1_ref, c2_ref)
    u = lax.broadcasted_iota(jnp.int32, (B_STEPS, 1, 1), 0)
    sn = lax.broadcasted_iota(jnp.int32, (s_len, 1, 1), 0)
    scores = []
    for g in range(ng):
        q = qkv_ref[0, s_idx, g]
        k_old = caches[g][:, 0]
        k_new = qkv_ref[0, :, ng + g]
        s_old = jnp.sum(k_old * q[None], axis=2, keepdims=True) * scale
        s_new = jnp.sum(k_new * q[None], axis=2, keepdims=True) * scale
        if B_GROUPS[g][1] == 1:
            s_old = jnp.where(u >= s_idx, s_old, NEG)
            s_new = jnp.where(sn <= s_idx, s_new, NEG)
        else:
            s_new = jnp.where(sn == s_idx, s_new, NEG)
        scores.append((s_old, s_new))
    mx = None
    for s_old, s_new in scores:
        cand = jnp.maximum(jnp.max(s_old, axis=0, keepdims=True), jnp.max(s_new, axis=0, keepdims=True))
        mx = cand if mx is None else jnp.maximum(mx, cand)
    den = jnp.zeros((1, B_HEADS, 1), F32)
    acc = jnp.zeros((B_HEADS, B_DH), F32)
    for g in range(ng):
        s_old, s_new = scores[g]
        p_old = jnp.exp(s_old - mx)
        p_new = jnp.exp(s_new - mx)
        den = den + jnp.sum(p_old, axis=0, keepdims=True) + jnp.sum(p_new, axis=0, keepdims=True)
        v_old = caches[g][:, 1]
        v_new = qkv_ref[0, :, 2 * ng + g]
        acc = acc + jnp.sum(p_old * v_old, axis=0) + jnp.sum(p_new * v_new, axis=0)
    o_ref[0] = (acc / den[0]).astype(o_ref.dtype)


def _sample_attention(qkv_s, c0, c1, c2):
    bsz, s_len = qkv_s.shape[:2]
    row = (B_STEPS, 2, B_HEADS, B_DH)
    return pl.pallas_call(
        functools.partial(_sattn_body, s_len=s_len),
        grid=(bsz, s_len),
        in_specs=[pl.BlockSpec((1,) + qkv_s.shape[1:], lambda b, s: (b, 0, 0, 0, 0)),
                  pl.BlockSpec((None, None) + row, lambda b, s: (0, b, 0, 0, 0, 0)),
                  pl.BlockSpec((None, None, B_STEPS, None) + row[1:], lambda b, s: (0, b, 0, s, 0, 0, 0)),
                  pl.BlockSpec((None, None, B_STEPS, None) + row[1:], lambda b, s: (0, b, 0, s, 0, 0, 0))],
        out_specs=pl.BlockSpec((1, B_HEADS, B_DH), lambda b, s: (b * s_len + s, 0, 0)),
        out_shape=jax.ShapeDtypeStruct((bsz * s_len, B_HEADS, B_DH), BF16),
        compiler_params=_cparams("parallel", "arbitrary"),
        name="sample_attention",
    )(qkv_s, c0, c1, c2)


def _gather_body(idx_ref, src_ref, *rest, k, tm, n_out, has_base):
    if has_base:
        base_ref, o_ref, buf, sem = rest
    else:
        o_ref, buf, sem = rest
    i = pl.program_id(0)

    def copy(j, r):
        row = idx_ref[j * n_out + i * tm + r]
        return pltpu.make_async_copy(src_ref.at[pl.ds(row, 1), :], buf.at[j, pl.ds(r, 1), :], sem.at[j])

    def issue(r, carry):
        for j in range(k):
            copy(j, r).start()
        return carry

    def wait(r, carry):
        for j in range(k):
            copy(j, r).wait()
        return carry

    lax.fori_loop(0, tm, issue, 0, unroll=8)
    lax.fori_loop(0, tm, wait, 0, unroll=8)
    acc = base_ref[...] if has_base else jnp.zeros(o_ref.shape, F32)
    for j in range(k):
        acc = acc + buf[j]
    o_ref[...] = acc.astype(o_ref.dtype)


def _gather_rows(src, idx, out_dtype, base=None):
    k, p = idx.shape
    tm = GATHER_TILE
    w = src.shape[1]
    in_specs = [pl.BlockSpec(memory_space=pl.ANY)]
    args = [src]
    if base is not None:
        in_specs.append(pl.BlockSpec((tm, w), lambda i, idx_ref: (i, 0)))
        args.append(base)
    return pl.pallas_call(
        functools.partial(_gather_body, k=k, tm=tm, n_out=p, has_base=base is not None),
        grid_spec=pltpu.PrefetchScalarGridSpec(
            num_scalar_prefetch=1,
            grid=(p // tm,),
            in_specs=in_specs,
            out_specs=pl.BlockSpec((tm, w), lambda i, idx_ref: (i, 0)),
            scratch_shapes=[pltpu.VMEM((k, tm, w), F32), pltpu.SemaphoreType.DMA((k,))]),
        out_shape=jax.ShapeDtypeStruct((p, w), out_dtype),
        compiler_params=_cparams("arbitrary"),
        name="gather_rows",
    )(idx.reshape(k * p), *args)


def _experts_body(te_ref, nu_ref, x_ref, g_ref, wg_ref, wu_ref, wd_ref, *rest, has_prev):
    if has_prev:
        prev_ref, o_ref = rest
    else:
        (o_ref,) = rest
    t = pl.program_id(0)

    @pl.when(t < nu_ref[0])
    def _():
        x = x_ref[...]
        a = jnp.dot(x, wg_ref[0, 0].astype(BF16), preferred_element_type=F32)
        b = jnp.dot(x, wu_ref[0, 0].astype(BF16), preferred_element_type=F32)
        hid = a * (1.0 / (1.0 + jnp.exp(-a))) * b * g_ref[...]
        y = jnp.dot(hid.astype(BF16), wd_ref[0, 0].astype(BF16), preferred_element_type=F32)
        if has_prev:
            y = y + prev_ref[...]
        o_ref[...] = y

    @pl.when(t >= nu_ref[0])
    def _():
        o_ref[...] = jnp.zeros(o_ref.shape, F32)


def _experts(xs, row_gate, tile_expert, n_used, w_gate, w_up, w_down, layer):
    p, d = xs.shape
    tm = MOE_TILE
    row = lambda t, te, nu: (t, 0)
    y = None
    for f in range(D_FF // FF_BLK):
        in_specs = [pl.BlockSpec((tm, d), row),
                    pl.BlockSpec((tm, 1), row),
                    pl.BlockSpec((1, 1, d, FF_BLK), lambda t, te, nu, f=f: (layer, te[t], 0, f)),
                    pl.BlockSpec((1, 1, d, FF_BLK), lambda t, te, nu, f=f: (layer, te[t], 0, f)),
                    pl.BlockSpec((1, 1, FF_BLK, d), lambda t, te, nu, f=f: (layer, te[t], f, 0))]
        args = [xs, row_gate, w_gate, w_up, w_down]
        if y is not None:
            in_specs.append(pl.BlockSpec((tm, d), row))
            args.append(y)
        y = pl.pallas_call(
            functools.partial(_experts_body, has_prev=y is not None),
            grid_spec=pltpu.PrefetchScalarGridSpec(
                num_scalar_prefetch=2,
                grid=(p // tm,),
                in_specs=in_specs,
                out_specs=pl.BlockSpec((tm, d), row)),
            out_shape=jax.ShapeDtypeStruct((p, d), F32),
            compiler_params=_cparams("arbitrary"),
            name="experts",
        )(tile_expert, n_used, *args)
    return y


def _dispatch_plan(e_idx, w_sel, n_rows):
    tm = MOE_TILE
    n_pair = e_idx.shape[0] * 2
    flat_e = e_idx.reshape(-1)
    onehot = (flat_e[:, None] == jnp.arange(N_EXPERTS, dtype=jnp.int32)[None, :]).astype(jnp.int32)
    incl = jnp.cumsum(onehot, axis=0)
    rank = jnp.sum((incl - onehot) * onehot, axis=1)
    counts = incl[-1]
    padded = ((counts + tm - 1) // tm) * tm
    ends = jnp.cumsum(padded)
    starts = ends - padded
    pos = jnp.sum(onehot * starts[None, :], axis=1) + rank
    row_token = jnp.zeros((n_rows,), jnp.int32).at[pos].set(jnp.arange(n_pair, dtype=jnp.int32) // 2)
    row_gate = jnp.zeros((n_rows,), F32).at[pos].set(w_sel.reshape(-1))
    n_tiles = n_rows // tm
    tile_start = jnp.arange(n_tiles, dtype=jnp.int32) * tm
    tile_expert = jnp.sum((tile_start[:, None] >= ends[None, :]).astype(jnp.int32), axis=1)
    n_used = (ends[-1] // tm).astype(jnp.int32)
    last_e = jnp.max(jnp.where(counts > 0, jnp.arange(N_EXPERTS, dtype=jnp.int32), 0))
    tile_expert = jnp.minimum(tile_expert, last_e).astype(jnp.int32)
    return pos.reshape(-1, 2), row_token, row_gate, tile_expert, n_used.reshape(1)


def _moe(x, g_norm, w_router, w_gate, w_up, w_down, layer):
    n, d = x.shape
    xn, route = _norm_route(x, g_norm, w_router)
    e_idx = route[:, :2].astype(jnp.int32)
    w_sel = route[:, 2:4]
    n_rows = ((2 * n + N_EXPERTS * (MOE_TILE - 1)) // MOE_TILE + 1) * MOE_TILE
    pos, row_token, row_gate, tile_expert, n_used = _dispatch_plan(e_idx, w_sel, n_rows)
    xs = _gather_rows(xn, row_token.reshape(1, n_rows), BF16)
    ys = _experts(xs, row_gate.reshape(n_rows, 1), tile_expert, n_used, w_gate, w_up, w_down, layer)
    return _gather_rows(ys, jnp.transpose(pos), F32, base=x)


def _rope_tables(bp, tp, bs, ts):
    half = B_DH // 2
    inv_freq = ROPE_THETA ** (-jnp.arange(half, dtype=F32) / half)
    pos = jnp.concatenate([jnp.tile(jnp.arange(tp, dtype=jnp.int32), bp),
                           jnp.tile(PAST_LEN + jnp.arange(ts, dtype=jnp.int32), bs)])
    ang = pos.astype(F32)[:, None] * inv_freq[None, :]
    cos = jnp.cos(ang)
    sin = jnp.sin(ang)
    return jnp.concatenate([cos, cos], axis=1), jnp.concatenate([-sin, sin], axis=1)


def kernel(x_prompt, x_sample, state_mlstm_C, state_mlstm_n, state_mlstm_m, cache_kv_w128, cache_kv_w512, cache_kv_w2048, g_norm_mix, g_norm_ffn, g_norm_final, w_in_a, b_gates_a, g_head_a, w_out_a, w_qkv_b, w_out_b, w_router_coarse, w_router_fine, w_gate_e, w_up_e, w_down_e):
    bp, tp, d = x_prompt.shape
    bs, ts, _ = x_sample.shape
    n_p, n_s = bp * tp, bs * ts
    x = jnp.concatenate([x_prompt.reshape(n_p, d), x_sample.reshape(n_s, d)], axis=0)

    def router_w(i):
        w = jnp.concatenate([w_router_coarse[i], w_router_fine[i]], axis=1)
        return jnp.pad(w, ((0, 0), (0, LANE - w.shape[1])))

    xn = _rmsnorm(x, g_norm_mix[0], BF16)
    w_in = w_in_a[0]
    proj = _matmul(xn, w_in, A_MAIN_W, 1024, BF16)
    w_g = jnp.pad(w_in[:, A_MAIN_W:], ((0, 0), (0, LANE - 2 * A_HEADS)))
    gates = _matmul(xn, w_g, LANE, LANE, F32)
    bias = jnp.pad(b_gates_a[0], (0, LANE - 2 * A_HEADS))
    bcol = bias.reshape(1, LANE)
    brow = jnp.broadcast_to(b_gates_a[0][:, None], (2 * A_HEADS, LANE))
    g_head = g_head_a[0].reshape(1, A_V_W)

    def gate_views(gt, bsz, t):
        gc = gt.reshape(bsz, t, LANE)
        return gc, jnp.transpose(gc[:, :, :2 * A_HEADS], (0, 2, 1))

    gc_p, gr_p = gate_views(gates[:n_p], bp, tp)
    zeros_c = jnp.zeros((bp, A_HEADS, A_DK, A_DV), F32)
    zeros_n = jnp.zeros((bp, A_HEADS, A_DK), F32)
    zeros_m = jnp.zeros((bp, A_HEADS, LANE), F32)
    h_p, c_p, nn_p, m_p = _mlstm(proj[:n_p].reshape(bp, tp, A_MAIN_W), gc_p, gr_p, bcol, brow, g_head,
                                 zeros_c, zeros_n, zeros_m, A_CHUNK_PROMPT, tp)
    ts_pad = 8
    pad_t = lambda a: jnp.pad(a, ((0, 0), (0, ts_pad - ts), (0, 0)))
    gc_s, gr_s = gate_views(pad_t(gates[n_p:].reshape(bs, ts, LANE)), bs, ts_pad)
    m0_s = jnp.broadcast_to(state_mlstm_m[0][:, :, None], (bs, A_HEADS, LANE))
    h_s, c_s, nn_s, m_s = _mlstm(pad_t(proj[n_p:].reshape(bs, ts, A_MAIN_W)), gc_s, gr_s, bcol, brow, g_head,
                                 state_mlstm_C[0], state_mlstm_n[0], m0_s, ts_pad, ts)
    h_all = jnp.concatenate([h_p.reshape(n_p, A_V_W), h_s[:, :ts].reshape(n_s, A_V_W)], axis=0)
    x = _matmul(h_all, w_out_a[0], d, 1024, F32, res=x)
    x = _moe(x, g_norm_ffn[0], router_w(0), w_gate_e, w_up_e, w_down_e, 0)

    ng = len(B_GROUPS)
    xn = _rmsnorm(x, g_norm_mix[1], BF16)
    cos, sin = _rope_tables(bp, tp, bs, ts)
    qkvh = _matmul(xn, w_qkv_b[0], B_QKV_W, 1024, F32, rot=(2 * ng * B_HW // 1024, cos, sin))
    attn_p = _prompt_attention(qkvh, _band_bias_tiles(), bp, tp)
    qkv_s = jnp.transpose(qkvh[:, n_p:], (1, 0, 2)).reshape(bs, ts, 3 * ng, B_HEADS, B_DH)
    caches = (cache_kv_w128, cache_kv_w512, cache_kv_w2048)
    cache_views = [c.reshape((1, bs, B_STEPS) + ((dil,) if dil > 1 else ()) + (2, B_HEADS, B_DH))
                   for c, (_, dil) in zip(caches, B_GROUPS)]
    attn_s = _sample_attention(qkv_s, *cache_views)
    attn = jnp.concatenate([attn_p, attn_s.reshape(n_s, B_HW)], axis=0)
    x = _matmul(attn, w_out_b[0], d, 1024, F32, res=x)
    x = _moe(x, g_norm_ffn[1], router_w(1), w_gate_e, w_up_e, w_down_e, 1)

    y_prompt = _rmsnorm(x, g_norm_final, F32, 0, n_p // ROW_TILE).reshape(bp, tp, d)
    y_sample = _rmsnorm(x, g_norm_final, F32, n_p // ROW_TILE, n_s // ROW_TILE).reshape(bs, ts, d)

    kvh_p = qkvh.reshape(3, ng, B_HEADS, n_p + n_s, B_DH)[1:, :, :, :n_p].reshape(2, ng, B_HEADS, bp, tp, B_DH)
    caches_p = [jnp.transpose(kvh_p[:, g, :, :, tp - min(win, tp):], (2, 3, 0, 1, 4))[None]
                for g, (win, _) in enumerate(B_GROUPS)]
    caches_s = [jnp.stack([qkv_s[:, :, ng + g], qkv_s[:, :, 2 * ng + g]], axis=2)[None] for g in range(ng)]
    return (y_prompt, y_sample,
            c_p[None], nn_p[None], m_p[:, :, 0][None],
            c_s[None], nn_s[None], m_s[:, :, 0][None],
            caches_p[0], caches_p[1], caches_p[2],
            caches_s[0], caches_s[1], caches_s[2])
```

```python
import functools

import jax
import jax.numpy as jnp
from jax import lax
from jax.experimental import pallas as pl
from jax.experimental.pallas import tpu as pltpu

F32 = jnp.float32
BF16 = jnp.bfloat16

D_MODEL = 2048
PAST_LEN = 2048
EPS = 1e-6
ROPE_THETA = 10000.0

A_HEADS = 8
A_DV = 256
A_DK = 128
A_QK_W = A_HEADS * A_DK
A_V_W = A_HEADS * A_DV
A_MAIN_W = 2 * A_QK_W + 2 * A_V_W
A_CHUNK_PROMPT = 256

B_GROUPS = ((128, 1), (512, 4), (2048, 16))
B_HEADS = 16
B_DH = 128
B_HW = B_HEADS * B_DH
B_QKV_W = 9 * B_HW
B_STEPS = 128
PA_BLK = 256
PA_PREV = tuple(-(-win // PA_BLK) for win, _ in B_GROUPS)
PA_NTILE = (2, 3, 2)
PA_TILE0 = (0, 2, 5)

N_EXPERTS = 16
MOE_GROUPS = 4
MOE_PER_GROUP = 4
D_FF = 1024
FF_BLK = 512

LANE = 128
ROW_TILE = 512
MM_TILE = 1088
MOE_TILE = 256
GATHER_TILE = 256
NEG = -1e30
VMEM_LIMIT = 52 * 1024 * 1024


def _cparams(*sem):
    return pltpu.CompilerParams(dimension_semantics=sem, vmem_limit_bytes=VMEM_LIMIT)


def _rms(x, g):
    return x * lax.rsqrt(jnp.mean(x * x, axis=-1, keepdims=True) + EPS) * g


def _rmsnorm_body(x_ref, g_ref, o_ref):
    o_ref[...] = _rms(x_ref[...], g_ref[...]).astype(o_ref.dtype)


def _rmsnorm(x, g, out_dtype, blk_off=0, n_blk=None):
    m, d = x.shape
    if n_blk is None:
        n_blk = m // ROW_TILE
    return pl.pallas_call(
        _rmsnorm_body,
        grid=(n_blk,),
        in_specs=[pl.BlockSpec((ROW_TILE, d), lambda i: (i + blk_off, 0)),
                  pl.BlockSpec((1, d), lambda i: (0, 0))],
        out_specs=pl.BlockSpec((ROW_TILE, d), lambda i: (i, 0)),
        out_shape=jax.ShapeDtypeStruct((n_blk * ROW_TILE, d), out_dtype),
        compiler_params=_cparams("parallel"),
        name="rmsnorm",
    )(x, g.reshape(1, d))


def _norm_route_body(x_ref, g_ref, wr_ref, xn_ref, route_ref):
    y = _rms(x_ref[...], g_ref[...])
    xn_ref[...] = y
    logits = jnp.dot(y.astype(BF16), wr_ref[...].astype(BF16), preferred_element_type=F32)
    lane = lax.broadcasted_iota(jnp.int32, logits.shape, 1)
    lane_f = lane.astype(F32)
    far = float(LANE)

    def first_max(vals):
        top = jnp.max(vals, axis=1, keepdims=True)
        return top, jnp.min(jnp.where(vals == top, lane_f, far), axis=1, keepdims=True)

    is_coarse = lane < MOE_GROUPS
    mc, grp = first_max(jnp.where(is_coarse, logits, NEG))
    p_grp = 1.0 / jnp.sum(jnp.where(is_coarse, jnp.exp(logits - mc), 0.0), axis=1, keepdims=True)
    lo = MOE_GROUPS + MOE_PER_GROUP * grp
    in_grp = jnp.logical_and(lane_f >= lo, lane_f < lo + MOE_PER_GROUP)
    fine = jnp.where(in_grp, logits, NEG)
    t1, i1 = first_max(fine)
    t2, i2 = first_max(jnp.where(lane_f == i1, NEG, fine))
    e = jnp.exp(t2 - t1)
    w1 = p_grp / (1.0 + e)
    w2 = w1 * e
    route = jnp.where(lane == 0, i1 - MOE_GROUPS,
                      jnp.where(lane == 1, i2 - MOE_GROUPS,
                                jnp.where(lane == 2, w1, jnp.where(lane == 3, w2, 0.0))))
    route_ref[...] = route


def _norm_route(x, g, w_router):
    m, d = x.shape
    return pl.pallas_call(
        _norm_route_body,
        grid=(m // ROW_TILE,),
        in_specs=[pl.BlockSpec((ROW_TILE, d), lambda i: (i, 0)),
                  pl.BlockSpec((1, d), lambda i: (0, 0)),
                  pl.BlockSpec((d, LANE), lambda i: (0, 0))],
        out_specs=[pl.BlockSpec((ROW_TILE, d), lambda i: (i, 0)),
                   pl.BlockSpec((ROW_TILE, LANE), lambda i: (i, 0))],
        out_shape=[jax.ShapeDtypeStruct((m, d), F32), jax.ShapeDtypeStruct((m, LANE), F32)],
        compiler_params=_cparams("parallel"),
        name="norm_route",
    )(x, g.reshape(1, d), w_router)


def _mm_body(*refs, tn, n_rot, has_res):
    it = iter(refs)
    x_ref = next(it)
    w_ref = next(it)
    cos_ref = next(it) if n_rot else None
    sin_ref = next(it) if n_rot else None
    res_ref = next(it) if has_res else None
    o_ref = next(it)
    w_scr = next(it)

    @pl.when(pl.program_id(1) == 0)
    def _():
        w_scr[...] = w_ref[...].astype(BF16)

    acc = jnp.dot(x_ref[...], w_scr[...], preferred_element_type=F32)
    if has_res:
        acc = acc + res_ref[...]
    if n_rot:
        j = pl.program_id(0)

        @pl.when(j < n_rot)
        def _():
            c = cos_ref[...]
            s = sin_ref[...]
            for h in range(tn // LANE):
                seg = acc[:, h * LANE:(h + 1) * LANE]
                rot = seg * c + pltpu.roll(seg, LANE // 2, 1) * s
                o_ref[h] = rot.astype(o_ref.dtype)

        @pl.when(j >= n_rot)
        def _():
            for h in range(tn // LANE):
                o_ref[h] = acc[:, h * LANE:(h + 1) * LANE].astype(o_ref.dtype)
    else:
        o_ref[...] = acc.astype(o_ref.dtype)


def _matmul(x, w, n_out, tn, out_dtype, res=None, rot=None, tm=ROW_TILE):
    m, k = x.shape
    assert m % tm == 0 and n_out % tn == 0
    grid = (n_out // tn, m // tm)
    in_specs = [pl.BlockSpec((tm, k), lambda j, i: (i, 0)),
                pl.BlockSpec((k, tn), lambda j, i: (0, j))]
    args = [x, w]
    n_rot = 0
    if rot is not None:
        n_rot, cos, sin = rot
        in_specs += [pl.BlockSpec((tm, LANE), lambda j, i: (i, 0))] * 2
        args += [cos, sin]
    if res is not None:
        in_specs.append(pl.BlockSpec((tm, tn), lambda j, i: (i, j)))
        args.append(res)
    if rot is not None:
        out_specs = pl.BlockSpec((tn // LANE, tm, LANE), lambda j, i: (j, i, 0))
        out_shape = jax.ShapeDtypeStruct((n_out // LANE, m, LANE), out_dtype)
    else:
        out_specs = pl.BlockSpec((tm, tn), lambda j, i: (i, j))
        out_shape = jax.ShapeDtypeStruct((m, n_out), out_dtype)
    return pl.pallas_call(
        functools.partial(_mm_body, tn=tn, n_rot=n_rot, has_res=res is not None),
        grid=grid,
        in_specs=in_specs,
        out_specs=out_specs,
        out_shape=out_shape,
        scratch_shapes=[pltpu.VMEM((k, tn), BF16)],
        compiler_params=_cparams("parallel", "arbitrary"),
        name="matmul",
    )(*args)


def _log_sigmoid(x):
    return jnp.minimum(x, 0.0) - jnp.log(1.0 + jnp.exp(-jnp.abs(x)))


def _mlstm_body(q_ref, k_ref, v_ref, o_ref, gc_ref, gr_ref, bc_ref, br_ref, gh_ref,
                c0_ref, n0_ref, m0_ref,
                h_ref, c1_ref, n1_ref, m1_ref,
                c_scr, n_scr, m_scr, *, L, t_valid):
    c_idx = pl.program_id(1)

    @pl.when(c_idx == 0)
    def _():
        c_scr[...] = c0_ref[0]
        n_scr[...] = n0_ref[0]
        m_scr[...] = m0_ref[0]

    scale = A_DK ** -0.5
    mm_dtype = BF16 if L >= 16 else F32
    gcol = gc_ref[0] + bc_ref[...]
    grow = gr_ref[0] + br_ref[:, :1]
    lane = lax.broadcasted_iota(jnp.int32, (L, LANE), 1)
    ti = lax.broadcasted_iota(jnp.int32, (L, L), 0)
    si = lax.broadcasted_iota(jnp.int32, (L, L), 1)
    causal = ti >= si
    pos0 = c_idx * L
    valid_c = (lax.broadcasted_iota(jnp.int32, (L, 1), 0) + pos0) < t_valid
    valid_r = (lax.broadcasted_iota(jnp.int32, (1, L), 1) + pos0) < t_valid

    for h in range(A_HEADS):
        q = q_ref[0, :, h * A_DK:(h + 1) * A_DK].astype(mm_dtype)
        k = k_ref[0, :, h * A_DK:(h + 1) * A_DK].astype(mm_dtype)
        v = v_ref[0, :, h * A_DV:(h + 1) * A_DV].astype(mm_dtype)
        ig_c = jnp.sum(jnp.where(lane == h, gcol, 0.0), axis=1, keepdims=True)
        fg_c = jnp.sum(jnp.where(lane == A_HEADS + h, gcol, 0.0), axis=1, keepdims=True)
        ig_r = grow[h:h + 1, :]
        fg_r = grow[A_HEADS + h:A_HEADS + h + 1, :]
        ig_c = jnp.where(valid_c, ig_c, NEG)
        ig_r = jnp.where(valid_r, ig_r, NEG)
        lf_c = jnp.where(valid_c, _log_sigmoid(fg_c), 0.0)
        lf_r = jnp.where(valid_r, _log_sigmoid(fg_r), 0.0)
        b_c = jnp.sum(jnp.where(causal, lf_r, 0.0), axis=1, keepdims=True)
        b_r = jnp.sum(jnp.where(ti <= si, lf_c, 0.0), axis=0, keepdims=True)
        m_prev = m_scr[h:h + 1, 0:1]
        c_prev = c_scr[h]
        n_prev = n_scr[h:h + 1, :]

        log_d = jnp.where(causal, b_c - b_r + ig_r, NEG)
        log_inter = b_c + m_prev
        m_t = jnp.maximum(log_inter, jnp.max(log_d, axis=1, keepdims=True))
        d = jnp.exp(log_d - m_t)
        inter = jnp.exp(log_inter - m_t)
        s = lax.dot_general(q, k, (((1,), (1,)), ((), ())), preferred_element_type=F32) * (scale * d)
        qc = jnp.dot(q, c_prev.astype(mm_dtype), preferred_element_type=F32) * scale
        sv = jnp.dot(s.astype(mm_dtype), v, preferred_element_type=F32)
        num = inter * qc + sv
        qn = jnp.sum(q.astype(F32) * n_prev, axis=1, keepdims=True) * scale
        den = inter * qn + jnp.sum(s, axis=1, keepdims=True)
        hh = num / jnp.maximum(jnp.abs(den), jnp.exp(-m_t))
        hh = hh * lax.rsqrt(jnp.mean(hh * hh, axis=1, keepdims=True) + EPS)
        og = o_ref[0, :, h * A_DV:(h + 1) * A_DV].astype(F32)
        hh = hh * gh_ref[:, h * A_DV:(h + 1) * A_DV] * (1.0 / (1.0 + jnp.exp(-og)))
        h_ref[0, :, h * A_DV:(h + 1) * A_DV] = hh.astype(h_ref.dtype)

        m_new = m_t[L - 1:L, :]
        b_last = b_c[L - 1:L, :]
        w_src = jnp.exp(b_last - b_c + ig_c - m_new)
        decay = jnp.exp(b_last + m_prev - m_new)
        kf = k.astype(F32)
        wv = w_src * v.astype(F32)
        kv = lax.dot_general(k, wv.astype(mm_dtype), (((0,), (0,)), ((), ())), preferred_element_type=F32)
        c_scr[h] = decay * c_prev + kv
        n_scr[h:h + 1, :] = decay * n_prev + jnp.sum(w_src * kf, axis=0, keepdims=True)
        m_scr[h:h + 1, :] = jnp.broadcast_to(m_new, (1, LANE))

    @pl.when(c_idx == pl.num_programs(1) - 1)
    def _():
        c1_ref[0] = c_scr[...]
        n1_ref[0] = n_scr[...]
        m1_ref[0] = m_scr[...]


def _mlstm(proj, gcol, grow, bcol, brow, g_head, c0, n0, m0, L, t_valid):
    bsz, t, _ = proj.shape
    nc = t // L
    state_specs = [pl.BlockSpec((1, A_HEADS, A_DK, A_DV), lambda b, c: (b, 0, 0, 0)),
                   pl.BlockSpec((1, A_HEADS, A_DK), lambda b, c: (b, 0, 0)),
                   pl.BlockSpec((1, A_HEADS, LANE), lambda b, c: (b, 0, 0))]
    return pl.pallas_call(
        functools.partial(_mlstm_body, L=L, t_valid=t_valid),
        grid=(bsz, nc),
        in_specs=[pl.BlockSpec((1, L, A_QK_W), lambda b, c: (b, c, 0)),
                  pl.BlockSpec((1, L, A_QK_W), lambda b, c: (b, c, 1)),
                  pl.BlockSpec((1, L, A_V_W), lambda b, c: (b, c, 1)),
                  pl.BlockSpec((1, L, A_V_W), lambda b, c: (b, c, 2)),
                  pl.BlockSpec((1, L, LANE), lambda b, c: (b, c, 0)),
                  pl.BlockSpec((1, 2 * A_HEADS, L), lambda b, c: (b, 0, c)),
                  pl.BlockSpec((1, LANE), lambda b, c: (0, 0)),
                  pl.BlockSpec((2 * A_HEADS, LANE), lambda b, c: (0, 0)),
                  pl.BlockSpec((1, A_V_W), lambda b, c: (0, 0))] + state_specs,
        out_specs=[pl.BlockSpec((1, L, A_V_W), lambda b, c: (b, c, 0))] + state_specs,
        out_shape=[jax.ShapeDtypeStruct((bsz, t, A_V_W), BF16),
                   jax.ShapeDtypeStruct((bsz, A_HEADS, A_DK, A_DV), F32),
                   jax.ShapeDtypeStruct((bsz, A_HEADS, A_DK), F32),
                   jax.ShapeDtypeStruct((bsz, A_HEADS, LANE), F32)],
        scratch_shapes=[pltpu.VMEM((A_HEADS, A_DK, A_DV), F32),
                        pltpu.VMEM((A_HEADS, A_DK), F32),
                        pltpu.VMEM((A_HEADS, LANE), F32)],
        compiler_params=_cparams("parallel", "arbitrary"),
        name="mlstm",
    )(proj, proj, proj, proj, gcol, grow, bcol, brow, g_head, c0, n0, m0)


def _band_bias_tiles():
    qi = jnp.arange(PA_BLK, dtype=jnp.int32)[:, None]
    kj = jnp.arange(PA_BLK, dtype=jnp.int32)[None, :]
    tiles = []
    for (win, dil), n in zip(B_GROUPS, PA_NTILE):
        for off in range(n):
            dist = off * PA_BLK + qi - kj
            ok = (dist >= 0) & (dist <= win) & (jnp.mod(dist, dil) == 0)
            tiles.append(jnp.where(ok, 0.0, NEG).astype(F32))
    return jnp.stack(tiles)


def _pattn_body(q0_ref, q1_ref, q2_ref, k0_ref, k1_ref, k2_ref, v0_ref, v1_ref, v2_ref,
                bias_ref, o_ref, kv_scr, s_scr, *, t):
    q_refs = (q0_ref, q1_ref, q2_ref)
    for i, r in enumerate((k0_ref, k1_ref, k2_ref, v0_ref, v1_ref, v2_ref)):
        kv_scr[i] = r[0].astype(BF16)
    scale = B_DH ** -0.5
    nt = (((1,), (1,)), ((), ()))
    for qb in range(t // PA_BLK):
        rows = slice(qb * PA_BLK, (qb + 1) * PA_BLK)
        chunks = []
        for g in range(len(B_GROUPS)):
            for off in range(min(PA_PREV[g], qb) + 1):
                chunks.append((g, qb - off, PA_TILE0[g] + min(off, PA_NTILE[g] - 1)))
        qs = [(r[0, rows, :] * scale).astype(BF16) for r in q_refs]
        mrun = None
        for c, (g, kb, tile) in enumerate(chunks):
            k = kv_scr[g, kb * PA_BLK:(kb + 1) * PA_BLK, :]
            s = lax.dot_general(qs[g], k, nt, preferred_element_type=F32) + bias_ref[tile]
            s_scr[:, c * PA_BLK:(c + 1) * PA_BLK] = s
            mrun = s if mrun is None else jnp.maximum(mrun, s)
        mx = jnp.max(mrun, axis=1, keepdims=True)
        lrun = None
        acc = jnp.zeros((PA_BLK, B_DH), F32)
        for c, (g, kb, tile) in enumerate(chunks):
            p = jnp.exp(s_scr[:, c * PA_BLK:(c + 1) * PA_BLK] - mx)
            lrun = p if lrun is None else lrun + p
            v = kv_scr[len(B_GROUPS) + g, kb * PA_BLK:(kb + 1) * PA_BLK, :]
            acc = acc + jnp.dot(p.astype(BF16), v, preferred_element_type=F32)
        den = jnp.sum(lrun, axis=1, keepdims=True)
        o_ref[rows, :] = (acc / den).astype(o_ref.dtype)


def _prompt_attention(qkvh, bias, bsz, t):
    ng = len(B_GROUPS)
    assert t % PA_BLK == 0 and t <= B_GROUPS[-1][0]

    def spec(which, g):
        return pl.BlockSpec((1, t, B_DH), lambda b, h: ((which * ng + g) * B_HEADS + h, b, 0))

    n_chunk = sum(min(p, t // PA_BLK - 1) + 1 for p in PA_PREV)
    return pl.pallas_call(
        functools.partial(_pattn_body, t=t),
        grid=(bsz, B_HEADS),
        in_specs=[spec(w, g) for w in range(3) for g in range(ng)]
        + [pl.BlockSpec((sum(PA_NTILE), PA_BLK, PA_BLK), lambda b, h: (0, 0, 0))],
        out_specs=pl.BlockSpec((t, B_DH), lambda b, h: (b, h)),
        out_shape=jax.ShapeDtypeStruct((bsz * t, B_HW), BF16),
        scratch_shapes=[pltpu.VMEM((2 * ng, t, B_DH), BF16),
                        pltpu.VMEM((PA_BLK, n_chunk * PA_BLK), F32)],
        compiler_params=_cparams("parallel", "parallel"),
        name="prompt_attention",
    )(*([qkvh] * 9), bias)


def _sattn_body(qkv_ref, c0_ref, c1_ref, c2_ref, o_ref, *, s_len):
    s_idx = pl.program_id(1)
    scale = B_DH ** -0.5
    ng = len(B_GROUPS)
    caches = (c0_ref, c1_ref, c2_ref)
    u = lax.broadcasted_iota(jnp.int32, (B_STEPS, 1, 1), 0)
    sn = lax.broadcasted_iota(jnp.int32, (s_len, 1, 1), 0)
    ones = jnp.ones((B_DH, LANE), F32)

    def head_dots(k, q):
        n = k.shape[0]
        prod = (k * q[None]).reshape(n * B_HEADS, B_DH)
        return jnp.dot(prod, ones, preferred_element_type=F32).reshape(n, B_HEADS, LANE)

    scores = []
    for g in range(ng):
        q = qkv_ref[0, s_idx, g] * scale
        s_old = head_dots(caches[g][:, 0], q)
        s_new = head_dots(qkv_ref[0, :, ng + g], q)
        if B_GROUPS[g][1] == 1:
            s_old = jnp.where(u >= s_idx, s_old, NEG)
            s_new = jnp.where(sn <= s_idx, s_new, NEG)
        else:
            s_new = jnp.where(sn == s_idx, s_new, NEG)
        scores.append((s_old, s_new))
    mx = None
    for s_old, s_new in scores:
        cand = jnp.maximum(jnp.max(s_old, axis=0, keepdims=True), jnp.max(s_new, axis=0, keepdims=True))
        mx = cand if mx is None else jnp.maximum(mx, cand)
    den = jnp.zeros((1, B_HEADS, LANE), F32)
    acc = jnp.zeros((B_HEADS, B_DH), F32)
    for g in range(ng):
        s_old, s_new = scores[g]
        p_old = jnp.exp(s_old - mx)
        p_new = jnp.exp(s_new - mx)
        den = den + jnp.sum(p_old, axis=0, keepdims=True) + jnp.sum(p_new, axis=0, keepdims=True)
        v_old = caches[g][:, 1]
        v_new = qkv_ref[0, :, 2 * ng + g]
        acc = acc + jnp.sum(p_old * v_old, axis=0) + jnp.sum(p_new * v_new, axis=0)
    o_ref[0] = (acc / den[0]).astype(o_ref.dtype)


def _sample_attention(qkv_s, c0, c1, c2):
    bsz, s_len = qkv_s.shape[:2]
    row = (B_STEPS, 2, B_HEADS, B_DH)
    return pl.pallas_call(
        functools.partial(_sattn_body, s_len=s_len),
        grid=(bsz, s_len),
        in_specs=[pl.BlockSpec((1,) + qkv_s.shape[1:], lambda b, s: (b, 0, 0, 0, 0)),
                  pl.BlockSpec((None, None) + row, lambda b, s: (0, b, 0, 0, 0, 0)),
                  pl.BlockSpec((None, None, B_STEPS, None) + row[1:], lambda b, s: (0, b, 0, s, 0, 0, 0)),
                  pl.BlockSpec((None, None, B_STEPS, None) + row[1:], lambda b, s: (0, b, 0, s, 0, 0, 0))],
        out_specs=pl.BlockSpec((1, B_HEADS, B_DH), lambda b, s: (b * s_len + s, 0, 0)),
        out_shape=jax.ShapeDtypeStruct((bsz * s_len, B_HEADS, B_DH), BF16),
        compiler_params=_cparams("parallel", "arbitrary"),
        name="sample_attention",
    )(qkv_s, c0, c1, c2)


def _gather_body(idx_ref, src_ref, *rest, k, tm, n_out, combine):
    if combine:
        base_ref, wt_ref, o_ref, buf, sem = rest
    else:
        o_ref, buf, sem = rest
    i = pl.program_id(0)

    def copy(step, slot, j, r):
        row = idx_ref[j * n_out + step * tm + r]
        return pltpu.make_async_copy(src_ref.at[pl.ds(row, 1), :], buf.at[slot, j, pl.ds(r, 1), :],
                                     sem.at[slot, j])

    def issue(step, slot):
        def body(r, carry):
            for j in range(k):
                copy(step, slot, j, r).start()
            return carry
        lax.fori_loop(0, tm, body, 0)

    @pl.when(i == 0)
    def _():
        issue(0, 0)

    @pl.when(i + 1 < pl.num_programs(0))
    def _():
        issue(i + 1, (i + 1) % 2)

    slot = i % 2

    def wait(r, carry):
        for j in range(k):
            copy(i, slot, j, r).wait()
        return carry

    lax.fori_loop(0, tm, wait, 0)
    if combine:
        acc = base_ref[...]
        for j in range(k):
            acc = acc + buf[slot, j] * wt_ref[:, j:j + 1]
    else:
        acc = buf[slot, 0]
        for j in range(1, k):
            acc = acc + buf[slot, j]
    o_ref[...] = acc.astype(o_ref.dtype)


def _gather_rows(src, idx, out_dtype, base=None, weights=None):
    k, p = idx.shape
    tm = GATHER_TILE
    w = src.shape[1]
    in_specs = [pl.BlockSpec(memory_space=pl.ANY)]
    args = [src]
    if base is not None:
        in_specs += [pl.BlockSpec((tm, w), lambda i, idx_ref: (i, 0)),
                     pl.BlockSpec((tm, k), lambda i, idx_ref: (i, 0))]
        args += [base, weights]
    return pl.pallas_call(
        functools.partial(_gather_body, k=k, tm=tm, n_out=p, combine=base is not None),
        grid_spec=pltpu.PrefetchScalarGridSpec(
            num_scalar_prefetch=1,
            grid=(p // tm,),
            in_specs=in_specs,
            out_specs=pl.BlockSpec((tm, w), lambda i, idx_ref: (i, 0)),
            scratch_shapes=[pltpu.VMEM((2, k, tm, w), F32), pltpu.SemaphoreType.DMA((2, k))]),
        out_shape=jax.ShapeDtypeStruct((p, w), out_dtype),
        compiler_params=_cparams("arbitrary"),
        name="gather_rows",
    )(idx.reshape(k * p), *args)


def _experts_body(te_ref, nu_ref, x_ref, wg_ref, wu_ref, wd_ref, *rest, has_prev):
    if has_prev:
        prev_ref, o_ref, wg_scr, wu_scr, wd_scr = rest
    else:
        o_ref, wg_scr, wu_scr, wd_scr = rest
    t = pl.program_id(0)

    @pl.when(jnp.logical_or(t == 0, te_ref[t] != te_ref[jnp.maximum(t - 1, 0)]))
    def _():
        wg_scr[...] = wg_ref[0, 0].astype(BF16)
        wu_scr[...] = wu_ref[0, 0].astype(BF16)
        wd_scr[...] = wd_ref[0, 0].astype(BF16)

    @pl.when(t < nu_ref[0])
    def _():
        x = x_ref[...]
        a = jnp.dot(x, wg_scr[...], preferred_element_type=F32)
        b = jnp.dot(x, wu_scr[...], preferred_element_type=F32)
        hid = a * (1.0 / (1.0 + jnp.exp(-a))) * b
        y = jnp.dot(hid.astype(BF16), wd_scr[...], preferred_element_type=F32)
        if has_prev:
            y = y + prev_ref[...]
        o_ref[...] = y

    @pl.when(t >= nu_ref[0])
    def _():
        o_ref[...] = jnp.zeros(o_ref.shape, F32)


def _experts(xs, tile_expert, n_used, w_gate, w_up, w_down, layer):
    p, d = xs.shape
    tm = MOE_TILE
    row = lambda t, te, nu: (t, 0)
    y = None
    for f in range(D_FF // FF_BLK):
        in_specs = [pl.BlockSpec((tm, d), row),
                    pl.BlockSpec((1, 1, d, FF_BLK), lambda t, te, nu, f=f: (layer, te[t], 0, f)),
                    pl.BlockSpec((1, 1, d, FF_BLK), lambda t, te, nu, f=f: (layer, te[t], 0, f)),
                    pl.BlockSpec((1, 1, FF_BLK, d), lambda t, te, nu, f=f: (layer, te[t], f, 0))]
        args = [xs, w_gate, w_up, w_down]
        if y is not None:
            in_specs.append(pl.BlockSpec((tm, d), row))
            args.append(y)
        y = pl.pallas_call(
            functools.partial(_experts_body, has_prev=y is not None),
            grid_spec=pltpu.PrefetchScalarGridSpec(
                num_scalar_prefetch=2,
                grid=(p // tm,),
                in_specs=in_specs,
                out_specs=pl.BlockSpec((tm, d), row),
                scratch_shapes=[pltpu.VMEM((d, FF_BLK), BF16), pltpu.VMEM((d, FF_BLK), BF16),
                                pltpu.VMEM((FF_BLK, d), BF16)]),
            out_shape=jax.ShapeDtypeStruct((p, d), F32),
            compiler_params=_cparams("arbitrary"),
            name="experts",
        )(tile_expert, n_used, *args)
    return y


def _dispatch_plan(e_idx, n_rows):
    tm = MOE_TILE
    n_pair = e_idx.shape[0] * 2
    flat_e = e_idx.reshape(-1)
    onehot = (flat_e[:, None] == jnp.arange(N_EXPERTS, dtype=jnp.int32)[None, :]).astype(jnp.int32)
    incl = jnp.cumsum(onehot, axis=0)
    rank = jnp.sum((incl - onehot) * onehot, axis=1)
    counts = incl[-1]
    padded = ((counts + tm - 1) // tm) * tm
    ends = jnp.cumsum(padded)
    starts = ends - padded
    pos = jnp.sum(onehot * starts[None, :], axis=1) + rank
    row_token = jnp.zeros((n_rows,), jnp.int32).at[pos].set(jnp.arange(n_pair, dtype=jnp.int32) // 2)
    n_tiles = n_rows // tm
    tile_start = jnp.arange(n_tiles, dtype=jnp.int32) * tm
    tile_expert = jnp.sum((tile_start[:, None] >= ends[None, :]).astype(jnp.int32), axis=1)
    n_used = (ends[-1] // tm).astype(jnp.int32)
    last_e = jnp.max(jnp.where(counts > 0, jnp.arange(N_EXPERTS, dtype=jnp.int32), 0))
    tile_expert = jnp.minimum(tile_expert, last_e).astype(jnp.int32)
    return pos.reshape(-1, 2), row_token, tile_expert, n_used.reshape(1)


def _moe(x, g_norm, w_router, w_gate, w_up, w_down, layer):
    n, d = x.shape
    xn, route = _norm_route(x, g_norm, w_router)
    e_idx = route[:, :2].astype(jnp.int32)
    w_sel = route[:, 2:4]
    n_rows = ((2 * n + N_EXPERTS * (MOE_TILE - 1)) // MOE_TILE + 1) * MOE_TILE
    pos, row_token, tile_expert, n_used = _dispatch_plan(e_idx, n_rows)
    xs = _gather_rows(xn, row_token.reshape(1, n_rows), BF16)
    ys = _experts(xs, tile_expert, n_used, w_gate, w_up, w_down, layer)
    return _gather_rows(ys, jnp.transpose(pos), F32, base=x, weights=w_sel)


def _rope_tables(bp, tp, bs, ts):
    half = B_DH // 2
    inv_freq = ROPE_THETA ** (-jnp.arange(half, dtype=F32) / half)
    pos = jnp.concatenate([jnp.tile(jnp.arange(tp, dtype=jnp.int32), bp),
                           jnp.tile(PAST_LEN + jnp.arange(ts, dtype=jnp.int32), bs)])
    ang = pos.astype(F32)[:, None] * inv_freq[None, :]
    cos = jnp.cos(ang)
    sin = jnp.sin(ang)
    return jnp.concatenate([cos, cos], axis=1), jnp.concatenate([-sin, sin], axis=1)


def kernel(x_prompt, x_sample, state_mlstm_C, state_mlstm_n, state_mlstm_m, cache_kv_w128, cache_kv_w512, cache_kv_w2048, g_norm_mix, g_norm_ffn, g_norm_final, w_in_a, b_gates_a, g_head_a, w_out_a, w_qkv_b, w_out_b, w_router_coarse, w_router_fine, w_gate_e, w_up_e, w_down_e):
    bp, tp, d = x_prompt.shape
    bs, ts, _ = x_sample.shape
    n_p, n_s = bp * tp, bs * ts
    x = jnp.concatenate([x_prompt.reshape(n_p, d), x_sample.reshape(n_s, d)], axis=0)

    def router_w(i):
        w = jnp.concatenate([w_router_coarse[i], w_router_fine[i]], axis=1)
        return jnp.pad(w, ((0, 0), (0, LANE - w.shape[1])))

    xn = _rmsnorm(x, g_norm_mix[0], BF16)
    w_in = w_in_a[0]
    proj = _matmul(xn, w_in, A_MAIN_W, 1024, BF16, tm=MM_TILE)
    w_g = jnp.pad(w_in[:, A_MAIN_W:], ((0, 0), (0, LANE - 2 * A_HEADS)))
    gates = _matmul(xn, w_g, LANE, LANE, F32)
    bias = jnp.pad(b_gates_a[0], (0, LANE - 2 * A_HEADS))
    bcol = bias.reshape(1, LANE)
    brow = jnp.broadcast_to(b_gates_a[0][:, None], (2 * A_HEADS, LANE))
    g_head = g_head_a[0].reshape(1, A_V_W)

    def gate_views(gt, bsz, t):
        gc = gt.reshape(bsz, t, LANE)
        return gc, jnp.transpose(gc[:, :, :2 * A_HEADS], (0, 2, 1))

    gc_p, gr_p = gate_views(gates[:n_p], bp, tp)
    zeros_c = jnp.zeros((bp, A_HEADS, A_DK, A_DV), F32)
    zeros_n = jnp.zeros((bp, A_HEADS, A_DK), F32)
    zeros_m = jnp.zeros((bp, A_HEADS, LANE), F32)
    h_p, c_p, nn_p, m_p = _mlstm(proj[:n_p].reshape(bp, tp, A_MAIN_W), gc_p, gr_p, bcol, brow, g_head,
                                 zeros_c, zeros_n, zeros_m, A_CHUNK_PROMPT, tp)
    ts_pad = 8
    pad_t = lambda a: jnp.pad(a, ((0, 0), (0, ts_pad - ts), (0, 0)))
    gc_s, gr_s = gate_views(pad_t(gates[n_p:].reshape(bs, ts, LANE)), bs, ts_pad)
    m0_s = jnp.broadcast_to(state_mlstm_m[0][:, :, None], (bs, A_HEADS, LANE))
    h_s, c_s, nn_s, m_s = _mlstm(pad_t(proj[n_p:].reshape(bs, ts, A_MAIN_W)), gc_s, gr_s, bcol, brow, g_head,
                                 state_mlstm_C[0], state_mlstm_n[0], m0_s, ts_pad, ts)
    h_all = jnp.concatenate([h_p.reshape(n_p, A_V_W), h_s[:, :ts].reshape(n_s, A_V_W)], axis=0)
    x = _matmul(h_all, w_out_a[0], d, 512, F32, res=x, tm=MM_TILE)
    x = _moe(x, g_norm_ffn[0], router_w(0), w_gate_e, w_up_e, w_down_e, 0)

    ng = len(B_GROUPS)
    xn = _rmsnorm(x, g_norm_mix[1], BF16)
    cos, sin = _rope_tables(bp, tp, bs, ts)
    qkvh = _matmul(xn, w_qkv_b[0], B_QKV_W, 1024, F32, rot=(2 * ng * B_HW // 1024, cos, sin), tm=MM_TILE)
    attn_p = _prompt_attention(qkvh, _band_bias_tiles(), bp, tp)
    qkv_s = jnp.transpose(qkvh[:, n_p:], (1, 0, 2)).reshape(bs, ts, 3 * ng, B_HEADS, B_DH)
    caches = (cache_kv_w128, cache_kv_w512, cache_kv_w2048)
    cache_views = [c.reshape((1, bs, B_STEPS) + ((dil,) if dil > 1 else ()) + (2, B_HEADS, B_DH))
                   for c, (_, dil) in zip(caches, B_GROUPS)]
    attn_s = _sample_attention(qkv_s, *cache_views)
    attn = jnp.concatenate([attn_p, attn_s.reshape(n_s, B_HW)], axis=0)
    x = _matmul(attn, w_out_b[0], d, 512, F32, res=x, tm=MM_TILE)
    x = _moe(x, g_norm_ffn[1], router_w(1), w_gate_e, w_up_e, w_down_e, 1)

    y_prompt = _rmsnorm(x, g_norm_final, F32, 0, n_p // ROW_TILE).reshape(bp, tp, d)
    y_sample = _rmsnorm(x, g_norm_final, F32, n_p // ROW_TILE, n_s // ROW_TILE).reshape(bs, ts, d)

    qh = qkvh.reshape(3, ng, B_HEADS, n_p + n_s, B_DH)

    def prompt_rows(g, keep):
        per_b = [qh[1:, g, :, (b + 1) * tp - keep:(b + 1) * tp] for b in range(bp)]
        return jnp.transpose(jnp.stack(per_b), (0, 3, 1, 2, 4))[None]

    caches_p = [prompt_rows(g, min(win, tp)) for g, (win, _) in enumerate(B_GROUPS)]
    caches_s = [jnp.stack([qkv_s[:, :, ng + g], qkv_s[:, :, 2 * ng + g]], axis=2)[None] for g in range(ng)]
    return (y_prompt, y_sample,
            c_p[None], nn_p[None], m_p[:, :, 0][None],
            c_s[None], nn_s[None], m_s[:, :, 0][None],
            caches_p[0], caches_p[1], caches_p[2],
            caches_s[0], caches_s[1], caches_s[2])
```

```python
import functools

import jax
import jax.numpy as jnp
from jax import lax
from jax.experimental import pallas as pl
from jax.experimental.pallas import tpu as pltpu

F32 = jnp.float32
BF16 = jnp.bfloat16

D_MODEL = 2048
PAST_LEN = 2048
EPS = 1e-6
ROPE_THETA = 10000.0

A_HEADS = 8
A_DV = 256
A_DK = 128
A_QK_W = A_HEADS * A_DK
A_V_W = A_HEADS * A_DV
A_MAIN_W = 2 * A_QK_W + 2 * A_V_W
A_CHUNK_PROMPT = 256

B_GROUPS = ((128, 1), (512, 4), (2048, 16))
B_HEADS = 16
B_DH = 128
B_HW = B_HEADS * B_DH
B_QKV_W = 9 * B_HW
B_STEPS = 128
SA_TOK = 4
PA_BLK = 256
PA_PREV = tuple(-(-win // PA_BLK) for win, _ in B_GROUPS)
PA_NTILE = (2, 3, 2)
PA_TILE0 = (0, 2, 5)

N_EXPERTS = 16
MOE_GROUPS = 4
MOE_PER_GROUP = 4
D_FF = 1024
FF_BLK = 512

LANE = 128
ROW_TILE = 512
MM_TILE = 1088
MOE_TILE = 256
GATHER_TILE = 256
NEG = -1e30
VMEM_LIMIT = 52 * 1024 * 1024


def _cparams(*sem):
    return pltpu.CompilerParams(dimension_semantics=sem, vmem_limit_bytes=VMEM_LIMIT)


def _rms(x, g):
    return x * lax.rsqrt(jnp.mean(x * x, axis=-1, keepdims=True) + EPS) * g


def _rmsnorm_body(x_ref, g_ref, o_ref):
    o_ref[...] = _rms(x_ref[...], g_ref[...]).astype(o_ref.dtype)


def _rmsnorm(x, g, out_dtype, blk_off=0, n_blk=None):
    m, d = x.shape
    if n_blk is None:
        n_blk = m // ROW_TILE
    return pl.pallas_call(
        _rmsnorm_body,
        grid=(n_blk,),
        in_specs=[pl.BlockSpec((ROW_TILE, d), lambda i: (i + blk_off, 0)),
                  pl.BlockSpec((1, d), lambda i: (0, 0))],
        out_specs=pl.BlockSpec((ROW_TILE, d), lambda i: (i, 0)),
        out_shape=jax.ShapeDtypeStruct((n_blk * ROW_TILE, d), out_dtype),
        compiler_params=_cparams("parallel"),
        name="rmsnorm",
    )(x, g.reshape(1, d))


def _norm_route_body(x_ref, g_ref, wr_ref, xn_ref, route_ref):
    y = _rms(x_ref[...], g_ref[...])
    xn_ref[...] = y
    logits = jnp.dot(y.astype(BF16), wr_ref[...].astype(BF16), preferred_element_type=F32)
    lane = lax.broadcasted_iota(jnp.int32, logits.shape, 1)
    lane_f = lane.astype(F32)
    far = float(LANE)

    def first_max(vals):
        top = jnp.max(vals, axis=1, keepdims=True)
        return top, jnp.min(jnp.where(vals == top, lane_f, far), axis=1, keepdims=True)

    is_coarse = lane < MOE_GROUPS
    mc, grp = first_max(jnp.where(is_coarse, logits, NEG))
    p_grp = 1.0 / jnp.sum(jnp.where(is_coarse, jnp.exp(logits - mc), 0.0), axis=1, keepdims=True)
    lo = MOE_GROUPS + MOE_PER_GROUP * grp
    in_grp = jnp.logical_and(lane_f >= lo, lane_f < lo + MOE_PER_GROUP)
    fine = jnp.where(in_grp, logits, NEG)
    t1, i1 = first_max(fine)
    t2, i2 = first_max(jnp.where(lane_f == i1, NEG, fine))
    e = jnp.exp(t2 - t1)
    w1 = p_grp / (1.0 + e)
    w2 = w1 * e
    route = jnp.where(lane == 0, i1 - MOE_GROUPS,
                      jnp.where(lane == 1, i2 - MOE_GROUPS,
                                jnp.where(lane == 2, w1, jnp.where(lane == 3, w2, 0.0))))
    route_ref[...] = route


def _norm_route(x, g, w_router):
    m, d = x.shape
    return pl.pallas_call(
        _norm_route_body,
        grid=(m // ROW_TILE,),
        in_specs=[pl.BlockSpec((ROW_TILE, d), lambda i: (i, 0)),
                  pl.BlockSpec((1, d), lambda i: (0, 0)),
                  pl.BlockSpec((d, LANE), lambda i: (0, 0))],
        out_specs=[pl.BlockSpec((ROW_TILE, d), lambda i: (i, 0)),
                   pl.BlockSpec((ROW_TILE, LANE), lambda i: (i, 0))],
        out_shape=[jax.ShapeDtypeStruct((m, d), F32), jax.ShapeDtypeStruct((m, LANE), F32)],
        compiler_params=_cparams("parallel"),
        name="norm_route",
    )(x, g.reshape(1, d), w_router)


def _mm_body(*refs, tn, n_rot, has_res):
    it = iter(refs)
    x_ref = next(it)
    w_ref = next(it)
    cos_ref = next(it) if n_rot else None
    sin_ref = next(it) if n_rot else None
    res_ref = next(it) if has_res else None
    o_ref = next(it)
    w_scr = next(it)

    @pl.when(pl.program_id(1) == 0)
    def _():
        w_scr[...] = w_ref[...].astype(BF16)

    acc = jnp.dot(x_ref[...], w_scr[...], preferred_element_type=F32)
    if has_res:
        acc = acc + res_ref[...]
    if n_rot:
        j = pl.program_id(0)

        @pl.when(j < n_rot)
        def _():
            c = cos_ref[...]
            s = sin_ref[...]
            for h in range(tn // LANE):
                seg = acc[:, h * LANE:(h + 1) * LANE]
                rot = seg * c + pltpu.roll(seg, LANE // 2, 1) * s
                o_ref[h] = rot.astype(o_ref.dtype)

        @pl.when(j >= n_rot)
        def _():
            for h in range(tn // LANE):
                o_ref[h] = acc[:, h * LANE:(h + 1) * LANE].astype(o_ref.dtype)
    else:
        o_ref[...] = acc.astype(o_ref.dtype)


def _matmul(x, w, n_out, tn, out_dtype, res=None, rot=None, tm=ROW_TILE):
    m, k = x.shape
    assert m % tm == 0 and n_out % tn == 0
    grid = (n_out // tn, m // tm)
    in_specs = [pl.BlockSpec((tm, k), lambda j, i: (i, 0)),
                pl.BlockSpec((k, tn), lambda j, i: (0, j))]
    args = [x, w]
    n_rot = 0
    if rot is not None:
        n_rot, cos, sin = rot
        in_specs += [pl.BlockSpec((tm, LANE), lambda j, i: (i, 0))] * 2
        args += [cos, sin]
    if res is not None:
        in_specs.append(pl.BlockSpec((tm, tn), lambda j, i: (i, j)))
        args.append(res)
    if rot is not None:
        out_specs = pl.BlockSpec((tn // LANE, tm, LANE), lambda j, i: (j, i, 0))
        out_shape = jax.ShapeDtypeStruct((n_out // LANE, m, LANE), out_dtype)
    else:
        out_specs = pl.BlockSpec((tm, tn), lambda j, i: (i, j))
        out_shape = jax.ShapeDtypeStruct((m, n_out), out_dtype)
    return pl.pallas_call(
        functools.partial(_mm_body, tn=tn, n_rot=n_rot, has_res=res is not None),
        grid=grid,
        in_specs=in_specs,
        out_specs=out_specs,
        out_shape=out_shape,
        scratch_shapes=[pltpu.VMEM((k, tn), BF16)],
        compiler_params=_cparams("parallel", "arbitrary"),
        name="matmul",
    )(*args)


def _log_sigmoid(x):
    return jnp.minimum(x, 0.0) - jnp.log(1.0 + jnp.exp(-jnp.abs(x)))


def _mlstm_body(q_ref, k_ref, v_ref, o_ref, gc_ref, gr_ref, bc_ref, br_ref, gh_ref,
                c0_ref, n0_ref, m0_ref,
                h_ref, c1_ref, n1_ref, m1_ref,
                c_scr, n_scr, m_scr, *, L, t_valid):
    c_idx = pl.program_id(1)

    @pl.when(c_idx == 0)
    def _():
        c_scr[...] = c0_ref[0]
        n_scr[...] = n0_ref[0]
        m_scr[...] = m0_ref[0]

    scale = A_DK ** -0.5
    mm_dtype = BF16 if L >= 16 else F32
    gcol = gc_ref[...] + bc_ref[...]
    grow = gr_ref[0] + br_ref[:, :1]
    lane = lax.broadcasted_iota(jnp.int32, (L, LANE), 1)
    ti = lax.broadcasted_iota(jnp.int32, (L, L), 0)
    si = lax.broadcasted_iota(jnp.int32, (L, L), 1)
    causal = ti >= si
    pos0 = c_idx * L
    valid_c = (lax.broadcasted_iota(jnp.int32, (L, 1), 0) + pos0) < t_valid
    valid_r = (lax.broadcasted_iota(jnp.int32, (1, L), 1) + pos0) < t_valid

    for h in range(A_HEADS):
        q = q_ref[:, h * A_DK:(h + 1) * A_DK].astype(mm_dtype)
        k = k_ref[:, h * A_DK:(h + 1) * A_DK].astype(mm_dtype)
        v = v_ref[:, h * A_DV:(h + 1) * A_DV].astype(mm_dtype)
        ig_c = jnp.sum(jnp.where(lane == h, gcol, 0.0), axis=1, keepdims=True)
        fg_c = jnp.sum(jnp.where(lane == A_HEADS + h, gcol, 0.0), axis=1, keepdims=True)
        ig_r = grow[h:h + 1, :]
        fg_r = grow[A_HEADS + h:A_HEADS + h + 1, :]
        ig_c = jnp.where(valid_c, ig_c, NEG)
        ig_r = jnp.where(valid_r, ig_r, NEG)
        lf_c = jnp.where(valid_c, _log_sigmoid(fg_c), 0.0)
        lf_r = jnp.where(valid_r, _log_sigmoid(fg_r), 0.0)
        b_c = jnp.sum(jnp.where(causal, lf_r, 0.0), axis=1, keepdims=True)
        b_r = jnp.sum(jnp.where(ti <= si, lf_c, 0.0), axis=0, keepdims=True)
        m_prev = m_scr[h:h + 1, 0:1]
        c_prev = c_scr[h]
        n_prev = n_scr[h:h + 1, :]

        log_d = jnp.where(causal, b_c - b_r + ig_r, NEG)
        log_inter = b_c + m_prev
        m_t = jnp.maximum(log_inter, jnp.max(log_d, axis=1, keepdims=True))
        d = jnp.exp(log_d - m_t)
        inter = jnp.exp(log_inter - m_t)
        s = lax.dot_general(q, k, (((1,), (1,)), ((), ())), preferred_element_type=F32) * (scale * d)
        qc = jnp.dot(q, c_prev.astype(mm_dtype), preferred_element_type=F32) * scale
        sv = jnp.dot(s.astype(mm_dtype), v, preferred_element_type=F32)
        num = inter * qc + sv
        qn = jnp.sum(q.astype(F32) * n_prev, axis=1, keepdims=True) * scale
        den = inter * qn + jnp.sum(s, axis=1, keepdims=True)
        hh = num / jnp.maximum(jnp.abs(den), jnp.exp(-m_t))
        hh = hh * lax.rsqrt(jnp.mean(hh * hh, axis=1, keepdims=True) + EPS)
        og = o_ref[:, h * A_DV:(h + 1) * A_DV].astype(F32)
        hh = hh * gh_ref[:, h * A_DV:(h + 1) * A_DV] * (1.0 / (1.0 + jnp.exp(-og)))
        h_ref[:, h * A_DV:(h + 1) * A_DV] = hh.astype(h_ref.dtype)

        m_new = m_t[L - 1:L, :]
        b_last = b_c[L - 1:L, :]
        w_src = jnp.exp(b_last - b_c + ig_c - m_new)
        decay = jnp.exp(b_last + m_prev - m_new)
        kf = k.astype(F32)
        wv = w_src * v.astype(F32)
        kv = lax.dot_general(k, wv.astype(mm_dtype), (((0,), (0,)), ((), ())), preferred_element_type=F32)
        c_scr[h] = decay * c_prev + kv
        n_scr[h:h + 1, :] = decay * n_prev + jnp.sum(w_src * kf, axis=0, keepdims=True)
        m_scr[h:h + 1, :] = jnp.broadcast_to(m_new, (1, LANE))

    @pl.when(c_idx == pl.num_programs(1) - 1)
    def _():
        c1_ref[0] = c_scr[...]
        n1_ref[0] = n_scr[...]
        m1_ref[0] = m_scr[...]


def _mlstm(proj, gcol, grow, bcol, brow, g_head, c0, n0, m0, L, t_valid):
    bsz, _, t = grow.shape
    nc = t // L
    state_specs = [pl.BlockSpec((1, A_HEADS, A_DK, A_DV), lambda b, c: (b, 0, 0, 0)),
                   pl.BlockSpec((1, A_HEADS, A_DK), lambda b, c: (b, 0, 0)),
                   pl.BlockSpec((1, A_HEADS, LANE), lambda b, c: (b, 0, 0))]
    return pl.pallas_call(
        functools.partial(_mlstm_body, L=L, t_valid=t_valid),
        grid=(bsz, nc),
        in_specs=[pl.BlockSpec((L, A_QK_W), lambda b, c: (b * nc + c, 0)),
                  pl.BlockSpec((L, A_QK_W), lambda b, c: (b * nc + c, 1)),
                  pl.BlockSpec((L, A_V_W), lambda b, c: (b * nc + c, 1)),
                  pl.BlockSpec((L, A_V_W), lambda b, c: (b * nc + c, 2)),
                  pl.BlockSpec((L, LANE), lambda b, c: (b * nc + c, 0)),
                  pl.BlockSpec((1, 2 * A_HEADS, L), lambda b, c: (b, 0, c)),
                  pl.BlockSpec((1, LANE), lambda b, c: (0, 0)),
                  pl.BlockSpec((2 * A_HEADS, LANE), lambda b, c: (0, 0)),
                  pl.BlockSpec((1, A_V_W), lambda b, c: (0, 0))] + state_specs,
        out_specs=[pl.BlockSpec((L, A_V_W), lambda b, c: (b * nc + c, 0))] + state_specs,
        out_shape=[jax.ShapeDtypeStruct((bsz * t, A_V_W), proj.dtype),
                   jax.ShapeDtypeStruct((bsz, A_HEADS, A_DK, A_DV), F32),
                   jax.ShapeDtypeStruct((bsz, A_HEADS, A_DK), F32),
                   jax.ShapeDtypeStruct((bsz, A_HEADS, LANE), F32)],
        scratch_shapes=[pltpu.VMEM((A_HEADS, A_DK, A_DV), F32),
                        pltpu.VMEM((A_HEADS, A_DK), F32),
                        pltpu.VMEM((A_HEADS, LANE), F32)],
        compiler_params=_cparams("parallel", "arbitrary"),
        name="mlstm",
    )(proj, proj, proj, proj, gcol, grow, bcol, brow, g_head, c0, n0, m0)


def _band_bias_tiles():
    qi = jnp.arange(PA_BLK, dtype=jnp.int32)[:, None]
    kj = jnp.arange(PA_BLK, dtype=jnp.int32)[None, :]
    tiles = []
    for (win, dil), n in zip(B_GROUPS, PA_NTILE):
        for off in range(n):
            dist = off * PA_BLK + qi - kj
            ok = (dist >= 0) & (dist <= win) & (jnp.mod(dist, dil) == 0)
            tiles.append(jnp.where(ok, 0.0, NEG).astype(F32))
    return jnp.stack(tiles)


def _pattn_body(q0_ref, q1_ref, q2_ref, k0_ref, k1_ref, k2_ref, v0_ref, v1_ref, v2_ref,
                bias_ref, o_ref, kv_scr, s_scr, *, t):
    q_refs = (q0_ref, q1_ref, q2_ref)
    for i, r in enumerate((k0_ref, k1_ref, k2_ref, v0_ref, v1_ref, v2_ref)):
        kv_scr[i] = r[0].astype(BF16)
    scale = B_DH ** -0.5
    nt = (((1,), (1,)), ((), ()))
    for qb in range(t // PA_BLK):
        rows = slice(qb * PA_BLK, (qb + 1) * PA_BLK)
        chunks = []
        for g in range(len(B_GROUPS)):
            for off in range(min(PA_PREV[g], qb) + 1):
                chunks.append((g, qb - off, PA_TILE0[g] + min(off, PA_NTILE[g] - 1)))
        qs = [(r[0, rows, :] * scale).astype(BF16) for r in q_refs]
        mrun = None
        for c, (g, kb, tile) in enumerate(chunks):
            k = kv_scr[g, kb * PA_BLK:(kb + 1) * PA_BLK, :]
            s = lax.dot_general(qs[g], k, nt, preferred_element_type=F32) + bias_ref[tile]
            s_scr[:, c * PA_BLK:(c + 1) * PA_BLK] = s
            mrun = s if mrun is None else jnp.maximum(mrun, s)
        mx = jnp.max(mrun, axis=1, keepdims=True)
        lrun = None
        acc = jnp.zeros((PA_BLK, B_DH), F32)
        for c, (g, kb, tile) in enumerate(chunks):
            p = jnp.exp(s_scr[:, c * PA_BLK:(c + 1) * PA_BLK] - mx)
            lrun = p if lrun is None else lrun + p
            v = kv_scr[len(B_GROUPS) + g, kb * PA_BLK:(kb + 1) * PA_BLK, :]
            acc = acc + jnp.dot(p.astype(BF16), v, preferred_element_type=F32)
        den = jnp.sum(lrun, axis=1, keepdims=True)
        o_ref[rows, :] = (acc / den).astype(o_ref.dtype)


def _prompt_attention(qkvh, bias, bsz, t):
    ng = len(B_GROUPS)
    assert t % PA_BLK == 0 and t <= B_GROUPS[-1][0]

    def spec(which, g):
        return pl.BlockSpec((1, t, B_DH), lambda b, h: ((which * ng + g) * B_HEADS + h, b, 0))

    n_chunk = sum(min(p, t // PA_BLK - 1) + 1 for p in PA_PREV)
    return pl.pallas_call(
        functools.partial(_pattn_body, t=t),
        grid=(bsz, B_HEADS),
        in_specs=[spec(w, g) for w in range(3) for g in range(ng)]
        + [pl.BlockSpec((sum(PA_NTILE), PA_BLK, PA_BLK), lambda b, h: (0, 0, 0))],
        out_specs=pl.BlockSpec((t, B_DH), lambda b, h: (b, h)),
        out_shape=jax.ShapeDtypeStruct((bsz * t, B_HW), BF16),
        scratch_shapes=[pltpu.VMEM((2 * ng, t, B_DH), BF16),
                        pltpu.VMEM((PA_BLK, n_chunk * PA_BLK), F32)],
        compiler_params=_cparams("parallel", "parallel"),
        name="prompt_attention",
    )(*([qkvh] * 9), bias)


def _sattn_body(qkv_ref, c0_ref, c1_ref, c2_ref, o_ref, *, s_len):
    scale = B_DH ** -0.5
    ng = len(B_GROUPS)
    caches = (c0_ref, c1_ref, c2_ref)
    u = lax.broadcasted_iota(jnp.int32, (B_STEPS, 1, 1), 0)
    sn = lax.broadcasted_iota(jnp.int32, (s_len, 1, 1), 0)
    ones = jnp.ones((B_DH, LANE), F32)

    def head_dots(k, q):
        n = k.shape[0]
        prod = (k * q[None]).reshape(n * B_HEADS, B_DH)
        return jnp.dot(prod, ones, preferred_element_type=F32).reshape(n, B_HEADS, LANE)

    for si in range(SA_TOK):
        s_idx = pl.program_id(1) * SA_TOK + si
        scores = []
        for g in range(ng):
            dense = B_GROUPS[g][1] == 1
            q = qkv_ref[0, s_idx, g] * scale
            k_old = caches[g][:, 0] if dense else caches[g][:, si, 0]
            s_old = head_dots(k_old, q)
            s_new = head_dots(qkv_ref[0, :, ng + g], q)
            if dense:
                s_old = jnp.where(u >= s_idx, s_old, NEG)
                s_new = jnp.where(sn <= s_idx, s_new, NEG)
            else:
                s_new = jnp.where(sn == s_idx, s_new, NEG)
            scores.append((s_old, s_new))
        mx = None
        for s_old, s_new in scores:
            cand = jnp.maximum(jnp.max(s_old, axis=0, keepdims=True), jnp.max(s_new, axis=0, keepdims=True))
            mx = cand if mx is None else jnp.maximum(mx, cand)
        den = jnp.zeros((1, B_HEADS, LANE), F32)
        acc = jnp.zeros((B_HEADS, B_DH), F32)
        for g in range(ng):
            s_old, s_new = scores[g]
            p_old = jnp.exp(s_old - mx)
            p_new = jnp.exp(s_new - mx)
            den = den + jnp.sum(p_old, axis=0, keepdims=True) + jnp.sum(p_new, axis=0, keepdims=True)
            v_old = caches[g][:, 1] if B_GROUPS[g][1] == 1 else caches[g][:, si, 1]
            v_new = qkv_ref[0, :, 2 * ng + g]
            acc = acc + jnp.sum(p_old * v_old, axis=0) + jnp.sum(p_new * v_new, axis=0)
        o_ref[si] = (acc / den[0]).astype(o_ref.dtype)


def _sample_attention(qkv_s, c0, c1, c2):
    bsz, s_len = qkv_s.shape[:2]
    n_sb = s_len // SA_TOK
    row = (2, B_HEADS, B_DH)
    strided = pl.BlockSpec((None, None, B_STEPS, SA_TOK) + row, lambda b, s: (0, b, 0, s, 0, 0, 0))
    return pl.pallas_call(
        functools.partial(_sattn_body, s_len=s_len),
        grid=(bsz, n_sb),
        in_specs=[pl.BlockSpec((1,) + qkv_s.shape[1:], lambda b, s: (b, 0, 0, 0, 0)),
                  pl.BlockSpec((None, None, B_STEPS) + row, lambda b, s: (0, b, 0, 0, 0, 0)),
                  strided, strided],
        out_specs=pl.BlockSpec((SA_TOK, B_HEADS, B_DH), lambda b, s: (b * n_sb + s, 0, 0)),
        out_shape=jax.ShapeDtypeStruct((bsz * s_len, B_HEADS, B_DH), BF16),
        compiler_params=_cparams("parallel", "arbitrary"),
        name="sample_attention",
    )(qkv_s, c0, c1, c2)


def _gather_body(idx_ref, src_ref, *rest, k, tm, n_out, combine, unroll):
    if combine:
        base_ref, wt_ref, o_ref, buf, sem = rest
    else:
        o_ref, buf, sem = rest
    i = pl.program_id(0)

    def copy(j, r):
        row = idx_ref[j * n_out + i * tm + r]
        return pltpu.make_async_copy(src_ref.at[pl.ds(row, 1), :], buf.at[j, pl.ds(r, 1), :], sem.at[j])

    def issue(r, carry):
        for j in range(k):
            copy(j, r).start()
        return carry

    def wait(r, carry):
        for j in range(k):
            copy(j, r).wait()
        return carry

    lax.fori_loop(0, tm, issue, 0, unroll=unroll)
    lax.fori_loop(0, tm, wait, 0, unroll=unroll)
    if combine:
        acc = base_ref[...]
        for j in range(k):
            acc = acc + buf[j] * wt_ref[:, j:j + 1]
    else:
        acc = buf[0]
        for j in range(1, k):
            acc = acc + buf[j]
    o_ref[...] = acc.astype(o_ref.dtype)


def _gather_rows(src, idx, out_dtype, base=None, weights=None):
    k, p = idx.shape
    tm = GATHER_TILE
    w = src.shape[1]
    in_specs = [pl.BlockSpec(memory_space=pl.ANY)]
    args = [src]
    if base is not None:
        in_specs += [pl.BlockSpec((tm, w), lambda i, idx_ref: (i, 0)),
                     pl.BlockSpec((tm, k), lambda i, idx_ref: (i, 0))]
        args += [base, weights]
    return pl.pallas_call(
        functools.partial(_gather_body, k=k, tm=tm, n_out=p, combine=base is not None,
                          unroll=8 if k > 1 else 1),
        grid_spec=pltpu.PrefetchScalarGridSpec(
            num_scalar_prefetch=1,
            grid=(p // tm,),
            in_specs=in_specs,
            out_specs=pl.BlockSpec((tm, w), lambda i, idx_ref: (i, 0)),
            scratch_shapes=[pltpu.VMEM((k, tm, w), F32), pltpu.SemaphoreType.DMA((k,))]),
        out_shape=jax.ShapeDtypeStruct((p, w), out_dtype),
        compiler_params=_cparams("arbitrary"),
        name="gather_rows",
    )(idx.reshape(k * p), *args)


def _experts_body(te_ref, nu_ref, x_ref, wg_ref, wu_ref, wd_ref, *rest, has_prev):
    if has_prev:
        prev_ref, o_ref, wg_scr, wu_scr, wd_scr = rest
    else:
        o_ref, wg_scr, wu_scr, wd_scr = rest
    t = pl.program_id(0)

    @pl.when(jnp.logical_or(t == 0, te_ref[t] != te_ref[jnp.maximum(t - 1, 0)]))
    def _():
        wg_scr[...] = wg_ref[0, 0].astype(BF16)
        wu_scr[...] = wu_ref[0, 0].astype(BF16)
        wd_scr[...] = wd_ref[0, 0].astype(BF16)

    @pl.when(t < nu_ref[0])
    def _():
        x = x_ref[...]
        a = jnp.dot(x, wg_scr[...], preferred_element_type=F32)
        b = jnp.dot(x, wu_scr[...], preferred_element_type=F32)
        hid = a * (1.0 / (1.0 + jnp.exp(-a))) * b
        y = jnp.dot(hid.astype(BF16), wd_scr[...], preferred_element_type=F32)
        if has_prev:
            y = y + prev_ref[...].astype(F32)
        o_ref[...] = y.astype(o_ref.dtype)

    @pl.when(t >= nu_ref[0])
    def _():
        o_ref[...] = jnp.zeros(o_ref.shape, o_ref.dtype)


def _experts(xs, tile_expert, n_used, w_gate, w_up, w_down, layer):
    p, d = xs.shape
    tm = MOE_TILE
    row = lambda t, te, nu: (t, 0)
    y = None
    for f in range(D_FF // FF_BLK):
        in_specs = [pl.BlockSpec((tm, d), row),
                    pl.BlockSpec((1, 1, d, FF_BLK), lambda t, te, nu, f=f: (layer, te[t], 0, f)),
                    pl.BlockSpec((1, 1, d, FF_BLK), lambda t, te, nu, f=f: (layer, te[t], 0, f)),
                    pl.BlockSpec((1, 1, FF_BLK, d), lambda t, te, nu, f=f: (layer, te[t], f, 0))]
        args = [xs, w_gate, w_up, w_down]
        if y is not None:
            in_specs.append(pl.BlockSpec((tm, d), row))
            args.append(y)
        y = pl.pallas_call(
            functools.partial(_experts_body, has_prev=y is not None),
            grid_spec=pltpu.PrefetchScalarGridSpec(
                num_scalar_prefetch=2,
                grid=(p // tm,),
                in_specs=in_specs,
                out_specs=pl.BlockSpec((tm, d), row),
                scratch_shapes=[pltpu.VMEM((d, FF_BLK), BF16), pltpu.VMEM((d, FF_BLK), BF16),
                                pltpu.VMEM((FF_BLK, d), BF16)]),
            out_shape=jax.ShapeDtypeStruct((p, d), F32 if f == D_FF // FF_BLK - 1 else BF16),
            compiler_params=_cparams("arbitrary"),
            name="experts",
        )(tile_expert, n_used, *args)
    return y


def _dispatch_plan(e_idx, n_rows):
    tm = MOE_TILE
    n_pair = e_idx.shape[0] * 2
    flat_e = e_idx.reshape(-1)
    onehot = (flat_e[:, None] == jnp.arange(N_EXPERTS, dtype=jnp.int32)[None, :]).astype(jnp.int32)
    incl = jnp.cumsum(onehot, axis=0)
    rank = jnp.sum((incl - onehot) * onehot, axis=1)
    counts = incl[-1]
    padded = ((counts + tm - 1) // tm) * tm
    ends = jnp.cumsum(padded)
    starts = ends - padded
    pos = jnp.sum(onehot * starts[None, :], axis=1) + rank
    row_token = jnp.zeros((n_rows,), jnp.int32).at[pos].set(jnp.arange(n_pair, dtype=jnp.int32) // 2)
    n_tiles = n_rows // tm
    tile_start = jnp.arange(n_tiles, dtype=jnp.int32) * tm
    tile_expert = jnp.sum((tile_start[:, None] >= ends[None, :]).astype(jnp.int32), axis=1)
    n_used = (ends[-1] // tm).astype(jnp.int32)
    last_e = jnp.max(jnp.where(counts > 0, jnp.arange(N_EXPERTS, dtype=jnp.int32), 0))
    tile_expert = jnp.minimum(tile_expert, last_e).astype(jnp.int32)
    return pos.reshape(-1, 2), row_token, tile_expert, n_used.reshape(1)


def _moe(x, g_norm, w_router, w_gate, w_up, w_down, layer):
    n, d = x.shape
    xn, route = _norm_route(x, g_norm, w_router)
    e_idx = route[:, :2].astype(jnp.int32)
    w_sel = route[:, 2:4]
    n_rows = ((2 * n + N_EXPERTS * (MOE_TILE - 1)) // MOE_TILE + 1) * MOE_TILE
    pos, row_token, tile_expert, n_used = _dispatch_plan(e_idx, n_rows)
    xs = _gather_rows(xn, row_token.reshape(1, n_rows), BF16)
    ys = _experts(xs, tile_expert, n_used, w_gate, w_up, w_down, layer)
    return _gather_rows(ys, jnp.transpose(pos), F32, base=x, weights=w_sel)


def _rope_tables(bp, tp, bs, ts):
    half = B_DH // 2
    inv_freq = ROPE_THETA ** (-jnp.arange(half, dtype=F32) / half)
    pos = jnp.concatenate([jnp.tile(jnp.arange(tp, dtype=jnp.int32), bp),
                           jnp.tile(PAST_LEN + jnp.arange(ts, dtype=jnp.int32), bs)])
    ang = pos.astype(F32)[:, None] * inv_freq[None, :]
    cos = jnp.cos(ang)
    sin = jnp.sin(ang)
    return jnp.concatenate([cos, cos], axis=1), jnp.concatenate([-sin, sin], axis=1)


def kernel(x_prompt, x_sample, state_mlstm_C, state_mlstm_n, state_mlstm_m, cache_kv_w128, cache_kv_w512, cache_kv_w2048, g_norm_mix, g_norm_ffn, g_norm_final, w_in_a, b_gates_a, g_head_a, w_out_a, w_qkv_b, w_out_b, w_router_coarse, w_router_fine, w_gate_e, w_up_e, w_down_e):
    bp, tp, d = x_prompt.shape
    bs, ts, _ = x_sample.shape
    n_p, n_s = bp * tp, bs * ts
    x = jnp.concatenate([x_prompt.reshape(n_p, d), x_sample.reshape(n_s, d)], axis=0)

    def router_w(i):
        w = jnp.concatenate([w_router_coarse[i], w_router_fine[i]], axis=1)
        return jnp.pad(w, ((0, 0), (0, LANE - w.shape[1])))

    xn = _rmsnorm(x, g_norm_mix[0], BF16)
    w_in = w_in_a[0]
    proj = _matmul(xn, w_in, A_MAIN_W, 1024, BF16, tm=MM_TILE)
    w_g = jnp.pad(w_in[:, A_MAIN_W:], ((0, 0), (0, LANE - 2 * A_HEADS)))
    gates = _matmul(xn, w_g, LANE, LANE, F32)
    bias = jnp.pad(b_gates_a[0], (0, LANE - 2 * A_HEADS))
    bcol = bias.reshape(1, LANE)
    brow = jnp.broadcast_to(b_gates_a[0][:, None], (2 * A_HEADS, LANE))
    g_head = g_head_a[0].reshape(1, A_V_W)

    def gate_rows(gc, bsz, t):
        return jnp.transpose(gc.reshape(bsz, t, LANE)[:, :, :2 * A_HEADS], (0, 2, 1))

    zeros_c = jnp.zeros((bp, A_HEADS, A_DK, A_DV), F32)
    zeros_n = jnp.zeros((bp, A_HEADS, A_DK), F32)
    zeros_m = jnp.zeros((bp, A_HEADS, LANE), F32)
    h_p, c_p, nn_p, m_p = _mlstm(proj, gates, gate_rows(gates[:n_p], bp, tp), bcol, brow, g_head,
                                 zeros_c, zeros_n, zeros_m, A_CHUNK_PROMPT, tp)
    ts_pad = 8

    def pad_t(a):
        a = jnp.pad(a.reshape(bs, ts, a.shape[-1]), ((0, 0), (0, ts_pad - ts), (0, 0)))
        return a.reshape(bs * ts_pad, a.shape[-1])

    gc_s = pad_t(gates[n_p:])
    m0_s = jnp.broadcast_to(state_mlstm_m[0][:, :, None], (bs, A_HEADS, LANE))
    h_s, c_s, nn_s, m_s = _mlstm(pad_t(proj[n_p:].astype(F32)), gc_s, gate_rows(gc_s, bs, ts_pad), bcol, brow,
                                 g_head, state_mlstm_C[0], state_mlstm_n[0], m0_s, ts_pad, ts)
    h_s = h_s.reshape(bs, ts_pad, A_V_W)[:, :ts].reshape(n_s, A_V_W).astype(BF16)
    h_all = jnp.concatenate([h_p, h_s], axis=0)
    x = _matmul(h_all, w_out_a[0], d, 1024, F32, res=x)
    x = _moe(x, g_norm_ffn[0], router_w(0), w_gate_e, w_up_e, w_down_e, 0)

    ng = len(B_GROUPS)
    xn = _rmsnorm(x, g_norm_mix[1], BF16)
    cos, sin = _rope_tables(bp, tp, bs, ts)
    qkvh = _matmul(xn, w_qkv_b[0], B_QKV_W, 1024, F32, rot=(2 * ng * B_HW // 1024, cos, sin), tm=MM_TILE)
    attn_p = _prompt_attention(qkvh, _band_bias_tiles(), bp, tp)
    qkv_s = jnp.transpose(qkvh[:, n_p:], (1, 0, 2)).reshape(bs, ts, 3 * ng, B_HEADS, B_DH)
    caches = (cache_kv_w128, cache_kv_w512, cache_kv_w2048)
    cache_views = [c.reshape((1, bs, B_STEPS) + ((dil,) if dil > 1 else ()) + (2, B_HEADS, B_DH))
                   for c, (_, dil) in zip(caches, B_GROUPS)]
    attn_s = _sample_attention(qkv_s, *cache_views)
    attn = jnp.concatenate([attn_p, attn_s.reshape(n_s, B_HW)], axis=0)
    x = _matmul(attn, w_out_b[0], d, 1024, F32, res=x)
    x = _moe(x, g_norm_ffn[1], router_w(1), w_gate_e, w_up_e, w_down_e, 1)

    y_prompt = _rmsnorm(x, g_norm_final, F32, 0, n_p // ROW_TILE).reshape(bp, tp, d)
    y_sample = _rmsnorm(x, g_norm_final, F32, n_p // ROW_TILE, n_s // ROW_TILE).reshape(bs, ts, d)

    qh = qkvh.reshape(3, ng, B_HEADS, n_p + n_s, B_DH)

    def prompt_rows(g, keep):
        per_b = [qh[1:, g, :, (b + 1) * tp - keep:(b + 1) * tp] for b in range(bp)]
        return jnp.transpose(jnp.stack(per_b), (0, 3, 1, 2, 4))[None]

    caches_p = [prompt_rows(g, min(win, tp)) for g, (win, _) in enumerate(B_GROUPS)]
    caches_s = [jnp.stack([qkv_s[:, :, ng + g], qkv_s[:, :, 2 * ng + g]], axis=2)[None] for g in range(ng)]
    return (y_prompt, y_sample,
            c_p[None], nn_p[None], m_p[:, :, 0][None],
            c_s[None], nn_s[None], m_s[:, :, 0][None],
            caches_p[0], caches_p[1], caches_p[2],
            caches_s[0], caches_s[1], caches_s[2])
```

```python
import functools

import jax
import jax.numpy as jnp
from jax import lax
from jax.experimental import pallas as pl
from jax.experimental.pallas import tpu as pltpu

F32 = jnp.float32
BF16 = jnp.bfloat16

D_MODEL = 2048
PAST_LEN = 2048
EPS = 1e-6
ROPE_THETA = 10000.0

A_HEADS = 8
A_DV = 256
A_DK = 128
A_QK_W = A_HEADS * A_DK
A_V_W = A_HEADS * A_DV
A_MAIN_W = 2 * A_QK_W + 2 * A_V_W
A_CHUNK_PROMPT = 256

B_GROUPS = ((128, 1), (512, 4), (2048, 16))
B_HEADS = 16
B_DH = 128
B_HW = B_HEADS * B_DH
B_QKV_W = 9 * B_HW
B_STEPS = 128
SA_TOK = 4
PA_BLK = 256
PA_PREV = tuple(-(-win // PA_BLK) for win, _ in B_GROUPS)
PA_NTILE = (2, 3, 2)
PA_TILE0 = (0, 2, 5)

N_EXPERTS = 16
MOE_GROUPS = 4
MOE_PER_GROUP = 4
D_FF = 1024
FF_BLK = 512

LANE = 128
ROW_TILE = 512
MM_TILE = 1088
MOE_TILE = 256
GATHER_TILE = 256
NEG = -1e30
VMEM_LIMIT = 52 * 1024 * 1024


def _cparams(*sem):
    return pltpu.CompilerParams(dimension_semantics=sem, vmem_limit_bytes=VMEM_LIMIT)


def _rms(x, g):
    return x * lax.rsqrt(jnp.mean(x * x, axis=-1, keepdims=True) + EPS) * g


def _rmsnorm_body(x_ref, g_ref, o_ref):
    o_ref[...] = _rms(x_ref[...], g_ref[...]).astype(o_ref.dtype)


def _rmsnorm(x, g, out_dtype, blk_off=0, n_blk=None):
    m, d = x.shape
    if n_blk is None:
        n_blk = m // ROW_TILE
    return pl.pallas_call(
        _rmsnorm_body,
        grid=(n_blk,),
        in_specs=[pl.BlockSpec((ROW_TILE, d), lambda i: (i + blk_off, 0)),
                  pl.BlockSpec((1, d), lambda i: (0, 0))],
        out_specs=pl.BlockSpec((ROW_TILE, d), lambda i: (i, 0)),
        out_shape=jax.ShapeDtypeStruct((n_blk * ROW_TILE, d), out_dtype),
        compiler_params=_cparams("parallel"),
        name="rmsnorm",
    )(x, g.reshape(1, d))


def _norm_route_body(x_ref, g_ref, wr_ref, xn_ref, route_ref):
    y = _rms(x_ref[...], g_ref[...])
    xn_ref[...] = y
    logits = jnp.dot(y.astype(BF16), wr_ref[...].astype(BF16), preferred_element_type=F32)
    lane = lax.broadcasted_iota(jnp.int32, logits.shape, 1)
    lane_f = lane.astype(F32)
    far = float(LANE)

    def first_max(vals):
        top = jnp.max(vals, axis=1, keepdims=True)
        return top, jnp.min(jnp.where(vals == top, lane_f, far), axis=1, keepdims=True)

    is_coarse = lane < MOE_GROUPS
    mc, grp = first_max(jnp.where(is_coarse, logits, NEG))
    p_grp = 1.0 / jnp.sum(jnp.where(is_coarse, jnp.exp(logits - mc), 0.0), axis=1, keepdims=True)
    lo = MOE_GROUPS + MOE_PER_GROUP * grp
    in_grp = jnp.logical_and(lane_f >= lo, lane_f < lo + MOE_PER_GROUP)
    fine = jnp.where(in_grp, logits, NEG)
    t1, i1 = first_max(fine)
    t2, i2 = first_max(jnp.where(lane_f == i1, NEG, fine))
    e = jnp.exp(t2 - t1)
    w1 = p_grp / (1.0 + e)
    w2 = w1 * e
    route = jnp.where(lane == 0, i1 - MOE_GROUPS,
                      jnp.where(lane == 1, i2 - MOE_GROUPS,
                                jnp.where(lane == 2, w1, jnp.where(lane == 3, w2, 0.0))))
    route_ref[...] = route


def _norm_route(x, g, w_router):
    m, d = x.shape
    return pl.pallas_call(
        _norm_route_body,
        grid=(m // ROW_TILE,),
        in_specs=[pl.BlockSpec((ROW_TILE, d), lambda i: (i, 0)),
                  pl.BlockSpec((1, d), lambda i: (0, 0)),
                  pl.BlockSpec((d, LANE), lambda i: (0, 0))],
        out_specs=[pl.BlockSpec((ROW_TILE, d), lambda i: (i, 0)),
                   pl.BlockSpec((ROW_TILE, LANE), lambda i: (i, 0))],
        out_shape=[jax.ShapeDtypeStruct((m, d), F32), jax.ShapeDtypeStruct((m, LANE), F32)],
        compiler_params=_cparams("parallel"),
        name="norm_route",
    )(x, g.reshape(1, d), w_router)


def _mm_body(*refs, tn, n_rot, has_res):
    it = iter(refs)
    x_ref = next(it)
    w_ref = next(it)
    cos_ref = next(it) if n_rot else None
    sin_ref = next(it) if n_rot else None
    res_ref = next(it) if has_res else None
    o_ref = next(it)
    w_scr = next(it)

    @pl.when(pl.program_id(1) == 0)
    def _():
        w_scr[...] = w_ref[...].astype(BF16)

    acc = jnp.dot(x_ref[...], w_scr[...], preferred_element_type=F32)
    if has_res:
        acc = acc + res_ref[...]
    if n_rot:
        j = pl.program_id(0)

        @pl.when(j < n_rot)
        def _():
            c = cos_ref[...]
            s = sin_ref[...]
            for h in range(tn // LANE):
                seg = acc[:, h * LANE:(h + 1) * LANE]
                rot = seg * c + pltpu.roll(seg, LANE // 2, 1) * s
                o_ref[h] = rot.astype(o_ref.dtype)

        @pl.when(j >= n_rot)
        def _():
            for h in range(tn // LANE):
                o_ref[h] = acc[:, h * LANE:(h + 1) * LANE].astype(o_ref.dtype)
    else:
        o_ref[...] = acc.astype(o_ref.dtype)


def _matmul(x, w, n_out, tn, out_dtype, res=None, rot=None, tm=ROW_TILE):
    m, k = x.shape
    assert m % tm == 0 and n_out % tn == 0
    grid = (n_out // tn, m // tm)
    in_specs = [pl.BlockSpec((tm, k), lambda j, i: (i, 0)),
                pl.BlockSpec((k, tn), lambda j, i: (0, j))]
    args = [x, w]
    n_rot = 0
    if rot is not None:
        n_rot, cos, sin = rot
        in_specs += [pl.BlockSpec((tm, LANE), lambda j, i: (i, 0))] * 2
        args += [cos, sin]
    if res is not None:
        in_specs.append(pl.BlockSpec((tm, tn), lambda j, i: (i, j)))
        args.append(res)
    if rot is not None:
        out_specs = pl.BlockSpec((tn // LANE, tm, LANE), lambda j, i: (j, i, 0))
        out_shape = jax.ShapeDtypeStruct((n_out // LANE, m, LANE), out_dtype)
    else:
        out_specs = pl.BlockSpec((tm, tn), lambda j, i: (i, j))
        out_shape = jax.ShapeDtypeStruct((m, n_out), out_dtype)
    return pl.pallas_call(
        functools.partial(_mm_body, tn=tn, n_rot=n_rot, has_res=res is not None),
        grid=grid,
        in_specs=in_specs,
        out_specs=out_specs,
        out_shape=out_shape,
        scratch_shapes=[pltpu.VMEM((k, tn), BF16)],
        compiler_params=_cparams("parallel", "arbitrary"),
        name="matmul",
    )(*args)


def _log_sigmoid(x):
    return jnp.minimum(x, 0.0) - jnp.log(1.0 + jnp.exp(-jnp.abs(x)))


def _mlstm_body(q_ref, k_ref, v_ref, o_ref, gc_ref, gr_ref, bc_ref, br_ref, gh_ref,
                c0_ref, n0_ref, m0_ref,
                h_ref, c1_ref, n1_ref, m1_ref,
                c_scr, n_scr, m_scr, *, L, t_valid):
    c_idx = pl.program_id(1)

    @pl.when(c_idx == 0)
    def _():
        c_scr[...] = c0_ref[0]
        n_scr[...] = n0_ref[0]
        m_scr[...] = m0_ref[0]

    scale = A_DK ** -0.5
    mm_dtype = BF16 if L >= 16 else F32
    gcol = gc_ref[...] + bc_ref[...]
    grow = gr_ref[0] + br_ref[:, :1]
    lane = lax.broadcasted_iota(jnp.int32, (L, LANE), 1)
    ti = lax.broadcasted_iota(jnp.int32, (L, L), 0)
    si = lax.broadcasted_iota(jnp.int32, (L, L), 1)
    causal = ti >= si
    pos0 = c_idx * L
    valid_c = (lax.broadcasted_iota(jnp.int32, (L, 1), 0) + pos0) < t_valid
    valid_r = (lax.broadcasted_iota(jnp.int32, (1, L), 1) + pos0) < t_valid

    for h in range(A_HEADS):
        q = q_ref[:, h * A_DK:(h + 1) * A_DK].astype(mm_dtype)
        k = k_ref[:, h * A_DK:(h + 1) * A_DK].astype(mm_dtype)
        v = v_ref[:, h * A_DV:(h + 1) * A_DV].astype(mm_dtype)
        ig_c = jnp.sum(jnp.where(lane == h, gcol, 0.0), axis=1, keepdims=True)
        fg_c = jnp.sum(jnp.where(lane == A_HEADS + h, gcol, 0.0), axis=1, keepdims=True)
        ig_r = grow[h:h + 1, :]
        fg_r = grow[A_HEADS + h:A_HEADS + h + 1, :]
        ig_c = jnp.where(valid_c, ig_c, NEG)
        ig_r = jnp.where(valid_r, ig_r, NEG)
        lf_c = jnp.where(valid_c, _log_sigmoid(fg_c), 0.0)
        lf_r = jnp.where(valid_r, _log_sigmoid(fg_r), 0.0)
        b_c = jnp.sum(jnp.where(causal, lf_r, 0.0), axis=1, keepdims=True)
        b_r = jnp.sum(jnp.where(ti <= si, lf_c, 0.0), axis=0, keepdims=True)
        m_prev = m_scr[h:h + 1, 0:1]
        c_prev = c_scr[h]
        n_prev = n_scr[h:h + 1, :]

        log_d = jnp.where(causal, b_c - b_r + ig_r, NEG)
        log_inter = b_c + m_prev
        m_t = jnp.maximum(log_inter, jnp.max(log_d, axis=1, keepdims=True))
        d = jnp.exp(log_d - m_t)
        inter = jnp.exp(log_inter - m_t)
        s = lax.dot_general(q, k, (((1,), (1,)), ((), ())), preferred_element_type=F32) * (scale * d)
        qc = jnp.dot(q, c_prev.astype(mm_dtype), preferred_element_type=F32) * scale
        sv = jnp.dot(s.astype(mm_dtype), v, preferred_element_type=F32)
        num = inter * qc + sv
        qn = jnp.sum(q.astype(F32) * n_prev, axis=1, keepdims=True) * scale
        den = inter * qn + jnp.sum(s, axis=1, keepdims=True)
        hh = num / jnp.maximum(jnp.abs(den), jnp.exp(-m_t))
        hh = hh * lax.rsqrt(jnp.mean(hh * hh, axis=1, keepdims=True) + EPS)
        og = o_ref[:, h * A_DV:(h + 1) * A_DV].astype(F32)
        hh = hh * gh_ref[:, h * A_DV:(h + 1) * A_DV] * (1.0 / (1.0 + jnp.exp(-og)))
        h_ref[:, h * A_DV:(h + 1) * A_DV] = hh.astype(h_ref.dtype)

        m_new = m_t[L - 1:L, :]
        b_last = b_c[L - 1:L, :]
        w_src = jnp.exp(b_last - b_c + ig_c - m_new)
        decay = jnp.exp(b_last + m_prev - m_new)
        kf = k.astype(F32)
        wv = w_src * v.astype(F32)
        kv = lax.dot_general(k, wv.astype(mm_dtype), (((0,), (0,)), ((), ())), preferred_element_type=F32)
        c_scr[h] = decay * c_prev + kv
        n_scr[h:h + 1, :] = decay * n_prev + jnp.sum(w_src * kf, axis=0, keepdims=True)
        m_scr[h:h + 1, :] = jnp.broadcast_to(m_new, (1, LANE))

    @pl.when(c_idx == pl.num_programs(1) - 1)
    def _():
        c1_ref[0] = c_scr[...]
        n1_ref[0] = n_scr[...]
        m1_ref[0] = m_scr[...]


def _mlstm(proj, gcol, grow, bcol, brow, g_head, c0, n0, m0, L, t_valid):
    bsz, _, t = grow.shape
    nc = t // L
    state_specs = [pl.BlockSpec((1, A_HEADS, A_DK, A_DV), lambda b, c: (b, 0, 0, 0)),
                   pl.BlockSpec((1, A_HEADS, A_DK), lambda b, c: (b, 0, 0)),
                   pl.BlockSpec((1, A_HEADS, LANE), lambda b, c: (b, 0, 0))]
    return pl.pallas_call(
        functools.partial(_mlstm_body, L=L, t_valid=t_valid),
        grid=(bsz, nc),
        in_specs=[pl.BlockSpec((L, A_QK_W), lambda b, c: (b * nc + c, 0)),
                  pl.BlockSpec((L, A_QK_W), lambda b, c: (b * nc + c, 1)),
                  pl.BlockSpec((L, A_V_W), lambda b, c: (b * nc + c, 1)),
                  pl.BlockSpec((L, A_V_W), lambda b, c: (b * nc + c, 2)),
                  pl.BlockSpec((L, LANE), lambda b, c: (b * nc + c, 0)),
                  pl.BlockSpec((1, 2 * A_HEADS, L), lambda b, c: (b, 0, c)),
                  pl.BlockSpec((1, LANE), lambda b, c: (0, 0)),
                  pl.BlockSpec((2 * A_HEADS, LANE), lambda b, c: (0, 0)),
                  pl.BlockSpec((1, A_V_W), lambda b, c: (0, 0))] + state_specs,
        out_specs=[pl.BlockSpec((L, A_V_W), lambda b, c: (b * nc + c, 0))] + state_specs,
        out_shape=[jax.ShapeDtypeStruct((bsz * t, A_V_W), proj.dtype),
                   jax.ShapeDtypeStruct((bsz, A_HEADS, A_DK, A_DV), F32),
                   jax.ShapeDtypeStruct((bsz, A_HEADS, A_DK), F32),
                   jax.ShapeDtypeStruct((bsz, A_HEADS, LANE), F32)],
        scratch_shapes=[pltpu.VMEM((A_HEADS, A_DK, A_DV), F32),
                        pltpu.VMEM((A_HEADS, A_DK), F32),
                        pltpu.VMEM((A_HEADS, LANE), F32)],
        compiler_params=_cparams("parallel", "arbitrary"),
        name="mlstm",
    )(proj, proj, proj, proj, gcol, grow, bcol, brow, g_head, c0, n0, m0)


def _band_bias_tiles():
    qi = jnp.arange(PA_BLK, dtype=jnp.int32)[:, None]
    kj = jnp.arange(PA_BLK, dtype=jnp.int32)[None, :]
    tiles = []
    for (win, dil), n in zip(B_GROUPS, PA_NTILE):
        for off in range(n):
            dist = off * PA_BLK + qi - kj
            ok = (dist >= 0) & (dist <= win) & (jnp.mod(dist, dil) == 0)
            tiles.append(jnp.where(ok, 0.0, NEG).astype(F32))
    return jnp.stack(tiles)


def _pattn_body(q0_ref, q1_ref, q2_ref, k0_ref, k1_ref, k2_ref, v0_ref, v1_ref, v2_ref,
                bias_ref, o_ref, kv_scr, s_scr, *, t):
    q_refs = (q0_ref, q1_ref, q2_ref)
    for i, r in enumerate((k0_ref, k1_ref, k2_ref, v0_ref, v1_ref, v2_ref)):
        kv_scr[i] = r[0].astype(BF16)
    scale = B_DH ** -0.5
    nt = (((1,), (1,)), ((), ()))
    for qb in range(t // PA_BLK):
        rows = slice(qb * PA_BLK, (qb + 1) * PA_BLK)
        chunks = []
        for g in range(len(B_GROUPS)):
            for off in range(min(PA_PREV[g], qb) + 1):
                chunks.append((g, qb - off, PA_TILE0[g] + min(off, PA_NTILE[g] - 1)))
        qs = [(r[0, rows, :] * scale).astype(BF16) for r in q_refs]
        mrun = None
        for c, (g, kb, tile) in enumerate(chunks):
            k = kv_scr[g, kb * PA_BLK:(kb + 1) * PA_BLK, :]
            s = lax.dot_general(qs[g], k, nt, preferred_element_type=F32) + bias_ref[tile]
            s_scr[:, c * PA_BLK:(c + 1) * PA_BLK] = s
            mrun = s if mrun is None else jnp.maximum(mrun, s)
        mx = jnp.max(mrun, axis=1, keepdims=True)
        lrun = None
        acc = jnp.zeros((PA_BLK, B_DH), F32)
        for c, (g, kb, tile) in enumerate(chunks):
            p = jnp.exp(s_scr[:, c * PA_BLK:(c + 1) * PA_BLK] - mx)
            lrun = p if lrun is None else lrun + p
            v = kv_scr[len(B_GROUPS) + g, kb * PA_BLK:(kb + 1) * PA_BLK, :]
            acc = acc + jnp.dot(p.astype(BF16), v, preferred_element_type=F32)
        den = jnp.sum(lrun, axis=1, keepdims=True)
        o_ref[rows, :] = (acc / den).astype(o_ref.dtype)


def _prompt_attention(qkvh, bias, bsz, t):
    ng = len(B_GROUPS)
    assert t % PA_BLK == 0 and t <= B_GROUPS[-1][0]

    def spec(which, g):
        return pl.BlockSpec((1, t, B_DH), lambda b, h: ((which * ng + g) * B_HEADS + h, b, 0))

    n_chunk = sum(min(p, t // PA_BLK - 1) + 1 for p in PA_PREV)
    return pl.pallas_call(
        functools.partial(_pattn_body, t=t),
        grid=(bsz, B_HEADS),
        in_specs=[spec(w, g) for w in range(3) for g in range(ng)]
        + [pl.BlockSpec((sum(PA_NTILE), PA_BLK, PA_BLK), lambda b, h: (0, 0, 0))],
        out_specs=pl.BlockSpec((t, B_DH), lambda b, h: (b, h)),
        out_shape=jax.ShapeDtypeStruct((bsz * t, B_HW), BF16),
        scratch_shapes=[pltpu.VMEM((2 * ng, t, B_DH), BF16),
                        pltpu.VMEM((PA_BLK, n_chunk * PA_BLK), F32)],
        compiler_params=_cparams("parallel", "parallel"),
        name="prompt_attention",
    )(*([qkvh] * 9), bias)


def _sattn_body(qkv_ref, c0_ref, c1_ref, c2_ref, o_ref, *, s_len):
    scale = B_DH ** -0.5
    ng = len(B_GROUPS)
    caches = (c0_ref, c1_ref, c2_ref)
    u = lax.broadcasted_iota(jnp.int32, (B_STEPS, 1, 1), 0)
    sn = lax.broadcasted_iota(jnp.int32, (s_len, 1, 1), 0)
    ones = jnp.ones((B_DH, LANE), F32)

    def head_dots(k, q):
        n = k.shape[0]
        prod = (k * q[None]).reshape(n * B_HEADS, B_DH)
        return jnp.dot(prod, ones, preferred_element_type=F32).reshape(n, B_HEADS, LANE)

    for si in range(SA_TOK):
        s_idx = pl.program_id(1) * SA_TOK + si
        scores = []
        for g in range(ng):
            dense = B_GROUPS[g][1] == 1
            q = qkv_ref[0, s_idx, g] * scale
            k_old = caches[g][:, 0] if dense else caches[g][:, si, 0]
            s_old = head_dots(k_old, q)
            s_new = head_dots(qkv_ref[0, :, ng + g], q)
            if dense:
                s_old = jnp.where(u >= s_idx, s_old, NEG)
                s_new = jnp.where(sn <= s_idx, s_new, NEG)
            else:
                s_new = jnp.where(sn == s_idx, s_new, NEG)
            scores.append((s_old, s_new))
        mx = None
        for s_old, s_new in scores:
            cand = jnp.maximum(jnp.max(s_old, axis=0, keepdims=True), jnp.max(s_new, axis=0, keepdims=True))
            mx = cand if mx is None else jnp.maximum(mx, cand)
        den = jnp.zeros((1, B_HEADS, LANE), F32)
        acc = jnp.zeros((B_HEADS, B_DH), F32)
        for g in range(ng):
            s_old, s_new = scores[g]
            p_old = jnp.exp(s_old - mx)
            p_new = jnp.exp(s_new - mx)
            den = den + jnp.sum(p_old, axis=0, keepdims=True) + jnp.sum(p_new, axis=0, keepdims=True)
            v_old = caches[g][:, 1] if B_GROUPS[g][1] == 1 else caches[g][:, si, 1]
            v_new = qkv_ref[0, :, 2 * ng + g]
            acc = acc + jnp.sum(p_old * v_old, axis=0) + jnp.sum(p_new * v_new, axis=0)
        o_ref[si] = (acc / den[0]).astype(o_ref.dtype)


def _sample_attention(qkv_s, c0, c1, c2):
    bsz, s_len = qkv_s.shape[:2]
    n_sb = s_len // SA_TOK
    row = (2, B_HEADS, B_DH)
    strided = pl.BlockSpec((None, None, B_STEPS, SA_TOK) + row, lambda b, s: (0, b, 0, s, 0, 0, 0))
    return pl.pallas_call(
        functools.partial(_sattn_body, s_len=s_len),
        grid=(bsz, n_sb),
        in_specs=[pl.BlockSpec((1,) + qkv_s.shape[1:], lambda b, s: (b, 0, 0, 0, 0)),
                  pl.BlockSpec((None, None, B_STEPS) + row, lambda b, s: (0, b, 0, 0, 0, 0)),
                  strided, strided],
        out_specs=pl.BlockSpec((SA_TOK, B_HEADS, B_DH), lambda b, s: (b * n_sb + s, 0, 0)),
        out_shape=jax.ShapeDtypeStruct((bsz * s_len, B_HEADS, B_DH), BF16),
        compiler_params=_cparams("parallel", "arbitrary"),
        name="sample_attention",
    )(qkv_s, c0, c1, c2)


def _gather_body(idx_ref, src_ref, *rest, k, tm, n_out, combine, unroll):
    if combine:
        base_ref, wt_ref, o_ref, buf, sem = rest
    else:
        o_ref, buf, sem = rest
    i = pl.program_id(0)

    def copy(j, r):
        row = idx_ref[j * n_out + i * tm + r]
        return pltpu.make_async_copy(src_ref.at[pl.ds(row, 1), :], buf.at[j, pl.ds(r, 1), :], sem.at[j])

    def issue(r, carry):
        for j in range(k):
            copy(j, r).start()
        return carry

    def wait(r, carry):
        for j in range(k):
            copy(j, r).wait()
        return carry

    lax.fori_loop(0, tm, issue, 0, unroll=unroll)
    lax.fori_loop(0, tm, wait, 0, unroll=unroll)
    if combine:
        acc = base_ref[...]
        for j in range(k):
            acc = acc + buf[j] * wt_ref[:, j:j + 1]
    else:
        acc = buf[0]
        for j in range(1, k):
            acc = acc + buf[j]
    o_ref[...] = acc.astype(o_ref.dtype)


def _gather_rows(src, idx, out_dtype, base=None, weights=None):
    k, p = idx.shape
    tm = GATHER_TILE
    w = src.shape[1]
    in_specs = [pl.BlockSpec(memory_space=pl.ANY)]
    args = [src]
    if base is not None:
        in_specs += [pl.BlockSpec((tm, w), lambda i, idx_ref: (i, 0)),
                     pl.BlockSpec((tm, k), lambda i, idx_ref: (i, 0))]
        args += [base, weights]
    return pl.pallas_call(
        functools.partial(_gather_body, k=k, tm=tm, n_out=p, combine=base is not None, unroll=8),
        grid_spec=pltpu.PrefetchScalarGridSpec(
            num_scalar_prefetch=1,
            grid=(p // tm,),
            in_specs=in_specs,
            out_specs=pl.BlockSpec((tm, w), lambda i, idx_ref: (i, 0)),
            scratch_shapes=[pltpu.VMEM((k, tm, w), F32), pltpu.SemaphoreType.DMA((k,))]),
        out_shape=jax.ShapeDtypeStruct((p, w), out_dtype),
        compiler_params=_cparams("arbitrary"),
        name="gather_rows",
    )(idx.reshape(k * p), *args)


def _experts_body(te_ref, nu_ref, x_ref, wg_ref, wu_ref, wd_ref, *rest, has_prev):
    if has_prev:
        prev_ref, o_ref, wg_scr, wu_scr, wd_scr = rest
    else:
        o_ref, wg_scr, wu_scr, wd_scr = rest
    t = pl.program_id(0)

    @pl.when(jnp.logical_or(t == 0, te_ref[t] != te_ref[jnp.maximum(t - 1, 0)]))
    def _():
        wg_scr[...] = wg_ref[0, 0].astype(BF16)
        wu_scr[...] = wu_ref[0, 0].astype(BF16)
        wd_scr[...] = wd_ref[0, 0].astype(BF16)

    @pl.when(t < nu_ref[0])
    def _():
        x = x_ref[...]
        a = jnp.dot(x, wg_scr[...], preferred_element_type=F32)
        b = jnp.dot(x, wu_scr[...], preferred_element_type=F32)
        hid = a * (1.0 / (1.0 + jnp.exp(-a))) * b
        y = jnp.dot(hid.astype(BF16), wd_scr[...], preferred_element_type=F32)
        if has_prev:
            y = y + prev_ref[...].astype(F32)
        o_ref[...] = y.astype(o_ref.dtype)

    @pl.when(t >= nu_ref[0])
    def _():
        o_ref[...] = jnp.zeros(o_ref.shape, o_ref.dtype)


def _experts(xs, tile_expert, n_used, w_gate, w_up, w_down, layer):
    p, d = xs.shape
    tm = MOE_TILE
    row = lambda t, te, nu: (t, 0)
    y = None
    for f in range(D_FF // FF_BLK):
        in_specs = [pl.BlockSpec((tm, d), row),
                    pl.BlockSpec((1, 1, d, FF_BLK), lambda t, te, nu, f=f: (layer, te[t], 0, f)),
                    pl.BlockSpec((1, 1, d, FF_BLK), lambda t, te, nu, f=f: (layer, te[t], 0, f)),
                    pl.BlockSpec((1, 1, FF_BLK, d), lambda t, te, nu, f=f: (layer, te[t], f, 0))]
        args = [xs, w_gate, w_up, w_down]
        if y is not None:
            in_specs.append(pl.BlockSpec((tm, d), row))
            args.append(y)
        y = pl.pallas_call(
            functools.partial(_experts_body, has_prev=y is not None),
            grid_spec=pltpu.PrefetchScalarGridSpec(
                num_scalar_prefetch=2,
                grid=(p // tm,),
                in_specs=in_specs,
                out_specs=pl.BlockSpec((tm, d), row),
                scratch_shapes=[pltpu.VMEM((d, FF_BLK), BF16), pltpu.VMEM((d, FF_BLK), BF16),
                                pltpu.VMEM((FF_BLK, d), BF16)]),
            out_shape=jax.ShapeDtypeStruct((p, d), F32 if f == D_FF // FF_BLK - 1 else BF16),
            compiler_params=_cparams("arbitrary"),
            name="experts",
        )(tile_expert, n_used, *args)
    return y


def _dispatch_plan(e_idx, n_rows):
    tm = MOE_TILE
    n_pair = e_idx.shape[0] * 2
    flat_e = e_idx.reshape(-1)
    onehot = (flat_e[:, None] == jnp.arange(N_EXPERTS, dtype=jnp.int32)[None, :]).astype(jnp.int32)
    incl = jnp.cumsum(onehot, axis=0)
    rank = jnp.sum((incl - onehot) * onehot, axis=1)
    counts = incl[-1]
    padded = ((counts + tm - 1) // tm) * tm
    ends = jnp.cumsum(padded)
    starts = ends - padded
    pos = jnp.sum(onehot * starts[None, :], axis=1) + rank
    filler = jnp.arange(n_rows, dtype=jnp.int32) % e_idx.shape[0]
    row_token = filler.at[pos].set(jnp.arange(n_pair, dtype=jnp.int32) // 2)
    n_tiles = n_rows // tm
    tile_start = jnp.arange(n_tiles, dtype=jnp.int32) * tm
    tile_expert = jnp.sum((tile_start[:, None] >= ends[None, :]).astype(jnp.int32), axis=1)
    n_used = (ends[-1] // tm).astype(jnp.int32)
    last_e = jnp.max(jnp.where(counts > 0, jnp.arange(N_EXPERTS, dtype=jnp.int32), 0))
    tile_expert = jnp.minimum(tile_expert, last_e).astype(jnp.int32)
    return pos.reshape(-1, 2), row_token, tile_expert, n_used.reshape(1)


def _moe(x, g_norm, w_router, w_gate, w_up, w_down, layer):
    n, d = x.shape
    xn, route = _norm_route(x, g_norm, w_router)
    e_idx = route[:, :2].astype(jnp.int32)
    w_sel = route[:, 2:4]
    n_rows = ((2 * n + N_EXPERTS * (MOE_TILE - 1)) // MOE_TILE + 1) * MOE_TILE
    pos, row_token, tile_expert, n_used = _dispatch_plan(e_idx, n_rows)
    xs = _gather_rows(xn, row_token.reshape(1, n_rows), BF16)
    ys = _experts(xs, tile_expert, n_used, w_gate, w_up, w_down, layer)
    return _gather_rows(ys, jnp.transpose(pos), F32, base=x, weights=w_sel)


def _rope_tables(bp, tp, bs, ts):
    half = B_DH // 2
    inv_freq = ROPE_THETA ** (-jnp.arange(half, dtype=F32) / half)
    pos = jnp.concatenate([jnp.tile(jnp.arange(tp, dtype=jnp.int32), bp),
                           jnp.tile(PAST_LEN + jnp.arange(ts, dtype=jnp.int32), bs)])
    ang = pos.astype(F32)[:, None] * inv_freq[None, :]
    cos = jnp.cos(ang)
    sin = jnp.sin(ang)
    return jnp.concatenate([cos, cos], axis=1), jnp.concatenate([-sin, sin], axis=1)


def kernel(x_prompt, x_sample, state_mlstm_C, state_mlstm_n, state_mlstm_m, cache_kv_w128, cache_kv_w512, cache_kv_w2048, g_norm_mix, g_norm_ffn, g_norm_final, w_in_a, b_gates_a, g_head_a, w_out_a, w_qkv_b, w_out_b, w_router_coarse, w_router_fine, w_gate_e, w_up_e, w_down_e):
    bp, tp, d = x_prompt.shape
    bs, ts, _ = x_sample.shape
    n_p, n_s = bp * tp, bs * ts
    x = jnp.concatenate([x_prompt.reshape(n_p, d), x_sample.reshape(n_s, d)], axis=0)

    def router_w(i):
        w = jnp.concatenate([w_router_coarse[i], w_router_fine[i]], axis=1)
        return jnp.pad(w, ((0, 0), (0, LANE - w.shape[1])))

    xn = _rmsnorm(x, g_norm_mix[0], BF16)
    w_in = w_in_a[0]
    proj = _matmul(xn, w_in, A_MAIN_W, 1024, BF16, tm=MM_TILE)
    w_g = jnp.pad(w_in[:, A_MAIN_W:], ((0, 0), (0, LANE - 2 * A_HEADS)))
    gates = _matmul(xn, w_g, LANE, LANE, F32)
    bias = jnp.pad(b_gates_a[0], (0, LANE - 2 * A_HEADS))
    bcol = bias.reshape(1, LANE)
    brow = jnp.broadcast_to(b_gates_a[0][:, None], (2 * A_HEADS, LANE))
    g_head = g_head_a[0].reshape(1, A_V_W)

    def gate_rows(gc, bsz, t):
        return jnp.transpose(gc.reshape(bsz, t, LANE)[:, :, :2 * A_HEADS], (0, 2, 1))

    zeros_c = jnp.zeros((bp, A_HEADS, A_DK, A_DV), F32)
    zeros_n = jnp.zeros((bp, A_HEADS, A_DK), F32)
    zeros_m = jnp.zeros((bp, A_HEADS, LANE), F32)
    h_p, c_p, nn_p, m_p = _mlstm(proj, gates, gate_rows(gates[:n_p], bp, tp), bcol, brow, g_head,
                                 zeros_c, zeros_n, zeros_m, A_CHUNK_PROMPT, tp)
    ts_pad = 8

    def pad_t(a):
        a = jnp.pad(a.reshape(bs, ts, a.shape[-1]), ((0, 0), (0, ts_pad - ts), (0, 0)))
        return a.reshape(bs * ts_pad, a.shape[-1])

    gc_s = pad_t(gates[n_p:])
    m0_s = jnp.broadcast_to(state_mlstm_m[0][:, :, None], (bs, A_HEADS, LANE))
    h_s, c_s, nn_s, m_s = _mlstm(pad_t(proj[n_p:].astype(F32)), gc_s, gate_rows(gc_s, bs, ts_pad), bcol, brow,
                                 g_head, state_mlstm_C[0], state_mlstm_n[0], m0_s, ts_pad, ts)
    h_s = h_s.reshape(bs, ts_pad, A_V_W)[:, :ts].reshape(n_s, A_V_W).astype(BF16)
    h_all = jnp.concatenate([h_p, h_s], axis=0)
    x = _matmul(h_all, w_out_a[0], d, 1024, F32, res=x)
    x = _moe(x, g_norm_ffn[0], router_w(0), w_gate_e, w_up_e, w_down_e, 0)

    ng = len(B_GROUPS)
    xn = _rmsnorm(x, g_norm_mix[1], BF16)
    cos, sin = _rope_tables(bp, tp, bs, ts)
    qkvh = _matmul(xn, w_qkv_b[0], B_QKV_W, 1024, F32, rot=(2 * ng * B_HW // 1024, cos, sin), tm=MM_TILE)
    attn_p = _prompt_attention(qkvh, _band_bias_tiles(), bp, tp)
    qkv_s = jnp.transpose(qkvh[:, n_p:], (1, 0, 2)).reshape(bs, ts, 3 * ng, B_HEADS, B_DH)
    caches = (cache_kv_w128, cache_kv_w512, cache_kv_w2048)
    cache_views = [c.reshape((1, bs, B_STEPS) + ((dil,) if dil > 1 else ()) + (2, B_HEADS, B_DH))
                   for c, (_, dil) in zip(caches, B_GROUPS)]
    attn_s = _sample_attention(qkv_s, *cache_views)
    attn = jnp.concatenate([attn_p, attn_s.reshape(n_s, B_HW)], axis=0)
    x = _matmul(attn, w_out_b[0], d, 1024, F32, res=x)
    x = _moe(x, g_norm_ffn[1], router_w(1), w_gate_e, w_up_e, w_down_e, 1)

    y_prompt = _rmsnorm(x, g_norm_final, F32, 0, n_p // ROW_TILE).reshape(bp, tp, d)
    y_sample = _rmsnorm(x, g_norm_final, F32, n_p // ROW_TILE, n_s // ROW_TILE).reshape(bs, ts, d)

    qh = qkvh.reshape(3, ng, B_HEADS, n_p + n_s, B_DH)

    def prompt_rows(g, keep):
        per_b = [qh[1:, g, :, (b + 1) * tp - keep:(b + 1) * tp] for b in range(bp)]
        return jnp.transpose(jnp.stack(per_b), (0, 3, 1, 2, 4))[None]

    caches_p = [prompt_rows(g, min(win, tp)) for g, (win, _) in enumerate(B_GROUPS)]
    caches_s = [jnp.stack([qkv_s[:, :, ng + g], qkv_s[:, :, 2 * ng + g]], axis=2)[None] for g in range(ng)]
    return (y_prompt, y_sample,
            c_p[None], nn_p[None], m_p[:, :, 0][None],
            c_s[None], nn_s[None], m_s[:, :, 0][None],
            caches_p[0], caches_p[1], caches_p[2],
            caches_s[0], caches_s[1], caches_s[2])
```

```python
import functools

import jax
import jax.numpy as jnp
from jax import lax
from jax.experimental import pallas as pl
from jax.experimental.pallas import tpu as pltpu

F32 = jnp.float32
BF16 = jnp.bfloat16

D_MODEL = 2048
PAST_LEN = 2048
EPS = 1e-6
ROPE_THETA = 10000.0

A_HEADS = 8
A_DV = 256
A_DK = 128
A_QK_W = A_HEADS * A_DK
A_V_W = A_HEADS * A_DV
A_MAIN_W = 2 * A_QK_W + 2 * A_V_W
A_CHUNK_PROMPT = 256

B_GROUPS = ((128, 1), (512, 4), (2048, 16))
B_HEADS = 16
B_DH = 128
B_HW = B_HEADS * B_DH
B_QKV_W = 9 * B_HW
B_STEPS = 128
SA_TOK = 4
PA_BLK = 256
PA_PREV = tuple(-(-win // PA_BLK) for win, _ in B_GROUPS)
PA_NTILE = (2, 3, 2)
PA_TILE0 = (0, 2, 5)

N_EXPERTS = 16
MOE_GROUPS = 4
MOE_PER_GROUP = 4
D_FF = 1024
FF_BLK = 512

LANE = 128
ROW_TILE = 512
MM_TILE = 1088
MOE_TILE = 256
GATHER_TILE = 256
NEG = -1e30
VMEM_LIMIT = 52 * 1024 * 1024


def _cparams(*sem):
    return pltpu.CompilerParams(dimension_semantics=sem, vmem_limit_bytes=VMEM_LIMIT)


def _rms(x, g):
    return x * lax.rsqrt(jnp.mean(x * x, axis=-1, keepdims=True) + EPS) * g


def _rmsnorm_body(x_ref, g_ref, o_ref):
    o_ref[...] = _rms(x_ref[...], g_ref[...]).astype(o_ref.dtype)


def _rmsnorm(x, g, out_dtype, blk_off=0, n_blk=None):
    m, d = x.shape
    if n_blk is None:
        n_blk = m // ROW_TILE
    return pl.pallas_call(
        _rmsnorm_body,
        grid=(n_blk,),
        in_specs=[pl.BlockSpec((ROW_TILE, d), lambda i: (i + blk_off, 0)),
                  pl.BlockSpec((1, d), lambda i: (0, 0))],
        out_specs=pl.BlockSpec((ROW_TILE, d), lambda i: (i, 0)),
        out_shape=jax.ShapeDtypeStruct((n_blk * ROW_TILE, d), out_dtype),
        compiler_params=_cparams("parallel"),
        name="rmsnorm",
    )(x, g.reshape(1, d))


def _norm_route_body(x_ref, g_ref, wr_ref, xn_ref, route_ref):
    y = _rms(x_ref[...], g_ref[...])
    xn_ref[...] = y
    logits = jnp.dot(y.astype(BF16), wr_ref[...].astype(BF16), preferred_element_type=F32)
    lane = lax.broadcasted_iota(jnp.int32, logits.shape, 1)
    lane_f = lane.astype(F32)
    far = float(LANE)

    def first_max(vals):
        top = jnp.max(vals, axis=1, keepdims=True)
        return top, jnp.min(jnp.where(vals == top, lane_f, far), axis=1, keepdims=True)

    is_coarse = lane < MOE_GROUPS
    mc, grp = first_max(jnp.where(is_coarse, logits, NEG))
    p_grp = 1.0 / jnp.sum(jnp.where(is_coarse, jnp.exp(logits - mc), 0.0), axis=1, keepdims=True)
    lo = MOE_GROUPS + MOE_PER_GROUP * grp
    in_grp = jnp.logical_and(lane_f >= lo, lane_f < lo + MOE_PER_GROUP)
    fine = jnp.where(in_grp, logits, NEG)
    t1, i1 = first_max(fine)
    t2, i2 = first_max(jnp.where(lane_f == i1, NEG, fine))
    e = jnp.exp(t2 - t1)
    w1 = p_grp / (1.0 + e)
    w2 = w1 * e
    route = jnp.where(lane == 0, i1 - MOE_GROUPS,
                      jnp.where(lane == 1, i2 - MOE_GROUPS,
                                jnp.where(lane == 2, w1, jnp.where(lane == 3, w2, 0.0))))
    route_ref[...] = route


def _norm_route(x, g, w_router):
    m, d = x.shape
    return pl.pallas_call(
        _norm_route_body,
        grid=(m // ROW_TILE,),
        in_specs=[pl.BlockSpec((ROW_TILE, d), lambda i: (i, 0)),
                  pl.BlockSpec((1, d), lambda i: (0, 0)),
                  pl.BlockSpec((d, LANE), lambda i: (0, 0))],
        out_specs=[pl.BlockSpec((ROW_TILE, d), lambda i: (i, 0)),
                   pl.BlockSpec((ROW_TILE, LANE), lambda i: (i, 0))],
        out_shape=[jax.ShapeDtypeStruct((m, d), F32), jax.ShapeDtypeStruct((m, LANE), F32)],
        compiler_params=_cparams("parallel"),
        name="norm_route",
    )(x, g.reshape(1, d), w_router)


def _mm_body(*refs, tn, n_rot, has_res):
    it = iter(refs)
    x_ref = next(it)
    w_ref = next(it)
    cos_ref = next(it) if n_rot else None
    sin_ref = next(it) if n_rot else None
    res_ref = next(it) if has_res else None
    o_ref = next(it)
    w_scr = next(it)

    @pl.when(pl.program_id(1) == 0)
    def _():
        w_scr[...] = w_ref[...].astype(BF16)

    acc = jnp.dot(x_ref[...], w_scr[...], preferred_element_type=F32)
    if has_res:
        acc = acc + res_ref[...]
    if n_rot:
        j = pl.program_id(0)

        @pl.when(j < n_rot)
        def _():
            c = cos_ref[...]
            s = sin_ref[...]
            for h in range(tn // LANE):
                seg = acc[:, h * LANE:(h + 1) * LANE]
                rot = seg * c + pltpu.roll(seg, LANE // 2, 1) * s
                o_ref[h] = rot.astype(o_ref.dtype)

        @pl.when(j >= n_rot)
        def _():
            for h in range(tn // LANE):
                o_ref[h] = acc[:, h * LANE:(h + 1) * LANE].astype(o_ref.dtype)
    else:
        o_ref[...] = acc.astype(o_ref.dtype)


def _matmul(x, w, n_out, tn, out_dtype, res=None, rot=None, tm=ROW_TILE):
    m, k = x.shape
    assert m % tm == 0 and n_out % tn == 0
    grid = (n_out // tn, m // tm)
    in_specs = [pl.BlockSpec((tm, k), lambda j, i: (i, 0)),
                pl.BlockSpec((k, tn), lambda j, i: (0, j))]
    args = [x, w]
    n_rot = 0
    if rot is not None:
        n_rot, cos, sin = rot
        in_specs += [pl.BlockSpec((tm, LANE), lambda j, i: (i, 0))] * 2
        args += [cos, sin]
    if res is not None:
        in_specs.append(pl.BlockSpec((tm, tn), lambda j, i: (i, j)))
        args.append(res)
    if rot is not None:
        out_specs = pl.BlockSpec((tn // LANE, tm, LANE), lambda j, i: (j, i, 0))
        out_shape = jax.ShapeDtypeStruct((n_out // LANE, m, LANE), out_dtype)
    else:
        out_specs = pl.BlockSpec((tm, tn), lambda j, i: (i, j))
        out_shape = jax.ShapeDtypeStruct((m, n_out), out_dtype)
    return pl.pallas_call(
        functools.partial(_mm_body, tn=tn, n_rot=n_rot, has_res=res is not None),
        grid=grid,
        in_specs=in_specs,
        out_specs=out_specs,
        out_shape=out_shape,
        scratch_shapes=[pltpu.VMEM((k, tn), BF16)],
        compiler_params=_cparams("parallel", "arbitrary"),
        name="matmul",
    )(*args)


def _log_sigmoid(x):
    return jnp.minimum(x, 0.0) - jnp.log(1.0 + jnp.exp(-jnp.abs(x)))


def _mlstm_body(q_ref, k_ref, v_ref, o_ref, gc_ref, gr_ref, bc_ref, br_ref, gh_ref,
                c0_ref, n0_ref, m0_ref,
                h_ref, c1_ref, n1_ref, m1_ref,
                c_scr, n_scr, m_scr, *, L, t_valid):
    c_idx = pl.program_id(1)

    @pl.when(c_idx == 0)
    def _():
        c_scr[...] = c0_ref[0]
        n_scr[...] = n0_ref[0]
        m_scr[...] = m0_ref[0]

    scale = A_DK ** -0.5
    mm_dtype = BF16 if L >= 16 else F32
    gcol = gc_ref[...] + bc_ref[...]
    grow = gr_ref[0] + br_ref[:, :1]
    lane = lax.broadcasted_iota(jnp.int32, (L, LANE), 1)
    ti = lax.broadcasted_iota(jnp.int32, (L, L), 0)
    si = lax.broadcasted_iota(jnp.int32, (L, L), 1)
    causal = ti >= si
    pos0 = c_idx * L
    valid_c = (lax.broadcasted_iota(jnp.int32, (L, 1), 0) + pos0) < t_valid
    valid_r = (lax.broadcasted_iota(jnp.int32, (1, L), 1) + pos0) < t_valid

    for h in range(A_HEADS):
        q = q_ref[:, h * A_DK:(h + 1) * A_DK].astype(mm_dtype)
        k = k_ref[:, h * A_DK:(h + 1) * A_DK].astype(mm_dtype)
        v = v_ref[:, h * A_DV:(h + 1) * A_DV].astype(mm_dtype)
        ig_c = jnp.sum(jnp.where(lane == h, gcol, 0.0), axis=1, keepdims=True)
        fg_c = jnp.sum(jnp.where(lane == A_HEADS + h, gcol, 0.0), axis=1, keepdims=True)
        ig_r = grow[h:h + 1, :]
        fg_r = grow[A_HEADS + h:A_HEADS + h + 1, :]
        ig_c = jnp.where(valid_c, ig_c, NEG)
        ig_r = jnp.where(valid_r, ig_r, NEG)
        lf_c = jnp.where(valid_c, _log_sigmoid(fg_c), 0.0)
        lf_r = jnp.where(valid_r, _log_sigmoid(fg_r), 0.0)
        b_c = jnp.sum(jnp.where(causal, lf_r, 0.0), axis=1, keepdims=True)
        b_r = jnp.sum(jnp.where(ti <= si, lf_c, 0.0), axis=0, keepdims=True)
        m_prev = m_scr[h:h + 1, 0:1]
        c_prev = c_scr[h]
        n_prev = n_scr[h:h + 1, :]

        log_d = jnp.where(causal, b_c - b_r + ig_r, NEG)
        log_inter = b_c + m_prev
        m_t = jnp.maximum(log_inter, jnp.max(log_d, axis=1, keepdims=True))
        d = jnp.exp(log_d - m_t)
        inter = jnp.exp(log_inter - m_t)
        s = lax.dot_general(q, k, (((1,), (1,)), ((), ())), preferred_element_type=F32) * (scale * d)
        qc = jnp.dot(q, c_prev.astype(mm_dtype), preferred_element_type=F32) * scale
        sv = jnp.dot(s.astype(mm_dtype), v, preferred_element_type=F32)
        num = inter * qc + sv
        qn = jnp.sum(q.astype(F32) * n_prev, axis=1, keepdims=True) * scale
        den = inter * qn + jnp.sum(s, axis=1, keepdims=True)
        hh = num / jnp.maximum(jnp.abs(den), jnp.exp(-m_t))
        hh = hh * lax.rsqrt(jnp.mean(hh * hh, axis=1, keepdims=True) + EPS)
        og = o_ref[:, h * A_DV:(h + 1) * A_DV].astype(F32)
        hh = hh * gh_ref[:, h * A_DV:(h + 1) * A_DV] * (1.0 / (1.0 + jnp.exp(-og)))
        h_ref[:, h * A_DV:(h + 1) * A_DV] = hh.astype(h_ref.dtype)

        m_new = m_t[L - 1:L, :]
        b_last = b_c[L - 1:L, :]
        w_src = jnp.exp(b_last - b_c + ig_c - m_new)
        decay = jnp.exp(b_last + m_prev - m_new)
        kf = k.astype(F32)
        wv = w_src * v.astype(F32)
        kv = lax.dot_general(k, wv.astype(mm_dtype), (((0,), (0,)), ((), ())), preferred_element_type=F32)
        c_scr[h] = decay * c_prev + kv
        n_scr[h:h + 1, :] = decay * n_prev + jnp.sum(w_src * kf, axis=0, keepdims=True)
        m_scr[h:h + 1, :] = jnp.broadcast_to(m_new, (1, LANE))

    @pl.when(c_idx == pl.num_programs(1) - 1)
    def _():
        c1_ref[0] = c_scr[...]
        n1_ref[0] = n_scr[...]
        m1_ref[0] = m_scr[...]


def _mlstm(proj, gcol, grow, bcol, brow, g_head, c0, n0, m0, L, t_valid):
    bsz, _, t = grow.shape
    nc = t // L
    state_specs = [pl.BlockSpec((1, A_HEADS, A_DK, A_DV), lambda b, c: (b, 0, 0, 0)),
                   pl.BlockSpec((1, A_HEADS, A_DK), lambda b, c: (b, 0, 0)),
                   pl.BlockSpec((1, A_HEADS, LANE), lambda b, c: (b, 0, 0))]
    return pl.pallas_call(
        functools.partial(_mlstm_body, L=L, t_valid=t_valid),
        grid=(bsz, nc),
        in_specs=[pl.BlockSpec((L, A_QK_W), lambda b, c: (b * nc + c, 0)),
                  pl.BlockSpec((L, A_QK_W), lambda b, c: (b * nc + c, 1)),
                  pl.BlockSpec((L, A_V_W), lambda b, c: (b * nc + c, 1)),
                  pl.BlockSpec((L, A_V_W), lambda b, c: (b * nc + c, 2)),
                  pl.BlockSpec((L, LANE), lambda b, c: (b * nc + c, 0)),
                  pl.BlockSpec((1, 2 * A_HEADS, L), lambda b, c: (b, 0, c)),
                  pl.BlockSpec((1, LANE), lambda b, c: (0, 0)),
                  pl.BlockSpec((2 * A_HEADS, LANE), lambda b, c: (0, 0)),
                  pl.BlockSpec((1, A_V_W), lambda b, c: (0, 0))] + state_specs,
        out_specs=[pl.BlockSpec((L, A_V_W), lambda b, c: (b * nc + c, 0))] + state_specs,
        out_shape=[jax.ShapeDtypeStruct((bsz * t, A_V_W), proj.dtype),
                   jax.ShapeDtypeStruct((bsz, A_HEADS, A_DK, A_DV), F32),
                   jax.ShapeDtypeStruct((bsz, A_HEADS, A_DK), F32),
                   jax.ShapeDtypeStruct((bsz, A_HEADS, LANE), F32)],
        scratch_shapes=[pltpu.VMEM((A_HEADS, A_DK, A_DV), F32),
                        pltpu.VMEM((A_HEADS, A_DK), F32),
                        pltpu.VMEM((A_HEADS, LANE), F32)],
        compiler_params=_cparams("parallel", "arbitrary"),
        name="mlstm",
    )(proj, proj, proj, proj, gcol, grow, bcol, brow, g_head, c0, n0, m0)


def _band_bias_tiles():
    qi = jnp.arange(PA_BLK, dtype=jnp.int32)[:, None]
    kj = jnp.arange(PA_BLK, dtype=jnp.int32)[None, :]
    tiles = []
    for (win, dil), n in zip(B_GROUPS, PA_NTILE):
        for off in range(n):
            dist = off * PA_BLK + qi - kj
            ok = (dist >= 0) & (dist <= win) & (jnp.mod(dist, dil) == 0)
            tiles.append(jnp.where(ok, 0.0, NEG).astype(F32))
    return jnp.stack(tiles)


def _pattn_body(q0_ref, q1_ref, q2_ref, k0_ref, k1_ref, k2_ref, v0_ref, v1_ref, v2_ref,
                bias_ref, o_ref, kv_scr, s_scr, *, t):
    q_refs = (q0_ref, q1_ref, q2_ref)
    for i, r in enumerate((k0_ref, k1_ref, k2_ref, v0_ref, v1_ref, v2_ref)):
        kv_scr[i] = r[0].astype(BF16)
    scale = B_DH ** -0.5
    nt = (((1,), (1,)), ((), ()))
    for qb in range(t // PA_BLK):
        rows = slice(qb * PA_BLK, (qb + 1) * PA_BLK)
        chunks = []
        for g in range(len(B_GROUPS)):
            for off in range(min(PA_PREV[g], qb) + 1):
                chunks.append((g, qb - off, PA_TILE0[g] + min(off, PA_NTILE[g] - 1)))
        qs = [(r[0, rows, :] * scale).astype(BF16) for r in q_refs]
        mrun = None
        for c, (g, kb, tile) in enumerate(chunks):
            k = kv_scr[g, kb * PA_BLK:(kb + 1) * PA_BLK, :]
            s = lax.dot_general(qs[g], k, nt, preferred_element_type=F32) + bias_ref[tile]
            s_scr[:, c * PA_BLK:(c + 1) * PA_BLK] = s
            mrun = s if mrun is None else jnp.maximum(mrun, s)
        mx = jnp.max(mrun, axis=1, keepdims=True)
        lrun = None
        acc = jnp.zeros((PA_BLK, B_DH), F32)
        for c, (g, kb, tile) in enumerate(chunks):
            p = jnp.exp(s_scr[:, c * PA_BLK:(c + 1) * PA_BLK] - mx)
            lrun = p if lrun is None else lrun + p
            v = kv_scr[len(B_GROUPS) + g, kb * PA_BLK:(kb + 1) * PA_BLK, :]
            acc = acc + jnp.dot(p.astype(BF16), v, preferred_element_type=F32)
        den = jnp.sum(lrun, axis=1, keepdims=True)
        o_ref[rows, :] = (acc / den).astype(o_ref.dtype)


def _prompt_attention(qkvh, bias, bsz, t):
    ng = len(B_GROUPS)
    assert t % PA_BLK == 0 and t <= B_GROUPS[-1][0]

    def spec(which, g):
        return pl.BlockSpec((1, t, B_DH), lambda b, h: ((which * ng + g) * B_HEADS + h, b, 0))

    n_chunk = sum(min(p, t // PA_BLK - 1) + 1 for p in PA_PREV)
    return pl.pallas_call(
        functools.partial(_pattn_body, t=t),
        grid=(bsz, B_HEADS),
        in_specs=[spec(w, g) for w in range(3) for g in range(ng)]
        + [pl.BlockSpec((sum(PA_NTILE), PA_BLK, PA_BLK), lambda b, h: (0, 0, 0))],
        out_specs=pl.BlockSpec((t, B_DH), lambda b, h: (b, h)),
        out_shape=jax.ShapeDtypeStruct((bsz * t, B_HW), BF16),
        scratch_shapes=[pltpu.VMEM((2 * ng, t, B_DH), BF16),
                        pltpu.VMEM((PA_BLK, n_chunk * PA_BLK), F32)],
        compiler_params=_cparams("parallel", "parallel"),
        name="prompt_attention",
    )(*([qkvh] * 9), bias)


def _sattn_body(qkv_ref, c0_ref, c1_ref, c2_ref, o_ref, *, s_len):
    scale = B_DH ** -0.5
    ng = len(B_GROUPS)
    caches = (c0_ref, c1_ref, c2_ref)
    u = lax.broadcasted_iota(jnp.int32, (B_STEPS, 1, 1), 0)
    sn = lax.broadcasted_iota(jnp.int32, (s_len, 1, 1), 0)
    ones = jnp.ones((B_DH, LANE), F32)

    def head_dots(k, q):
        n = k.shape[0]
        prod = (k * q[None]).reshape(n * B_HEADS, B_DH)
        return jnp.dot(prod, ones, preferred_element_type=F32).reshape(n, B_HEADS, LANE)

    for si in range(SA_TOK):
        s_idx = pl.program_id(1) * SA_TOK + si
        scores = []
        for g in range(ng):
            dense = B_GROUPS[g][1] == 1
            q = qkv_ref[0, s_idx, g] * scale
            k_old = caches[g][:, 0] if dense else caches[g][:, si, 0]
            s_old = head_dots(k_old, q)
            s_new = head_dots(qkv_ref[0, :, ng + g], q)
            if dense:
                s_old = jnp.where(u >= s_idx, s_old, NEG)
                s_new = jnp.where(sn <= s_idx, s_new, NEG)
            else:
                s_new = jnp.where(sn == s_idx, s_new, NEG)
            scores.append((s_old, s_new))
        mx = None
        for s_old, s_new in scores:
            cand = jnp.maximum(jnp.max(s_old, axis=0, keepdims=True), jnp.max(s_new, axis=0, keepdims=True))
            mx = cand if mx is None else jnp.maximum(mx, cand)
        den = jnp.zeros((1, B_HEADS, LANE), F32)
        acc = jnp.zeros((B_HEADS, B_DH), F32)
        for g in range(ng):
            s_old, s_new = scores[g]
            p_old = jnp.exp(s_old - mx)
            p_new = jnp.exp(s_new - mx)
            den = den + jnp.sum(p_old, axis=0, keepdims=True) + jnp.sum(p_new, axis=0, keepdims=True)
            v_old = caches[g][:, 1] if B_GROUPS[g][1] == 1 else caches[g][:, si, 1]
            v_new = qkv_ref[0, :, 2 * ng + g]
            acc = acc + jnp.sum(p_old * v_old, axis=0) + jnp.sum(p_new * v_new, axis=0)
        o_ref[si] = (acc / den[0]).astype(o_ref.dtype)


def _sample_attention(qkv_s, c0, c1, c2):
    bsz, s_len = qkv_s.shape[:2]
    n_sb = s_len // SA_TOK
    row = (2, B_HEADS, B_DH)
    strided = pl.BlockSpec((None, None, B_STEPS, SA_TOK) + row, lambda b, s: (0, b, 0, s, 0, 0, 0))
    return pl.pallas_call(
        functools.partial(_sattn_body, s_len=s_len),
        grid=(bsz, n_sb),
        in_specs=[pl.BlockSpec((1,) + qkv_s.shape[1:], lambda b, s: (b, 0, 0, 0, 0)),
                  pl.BlockSpec((None, None, B_STEPS) + row, lambda b, s: (0, b, 0, 0, 0, 0)),
                  strided, strided],
        out_specs=pl.BlockSpec((SA_TOK, B_HEADS, B_DH), lambda b, s: (b * n_sb + s, 0, 0)),
        out_shape=jax.ShapeDtypeStruct((bsz * s_len, B_HEADS, B_DH), BF16),
        compiler_params=_cparams("parallel", "arbitrary"),
        name="sample_attention",
    )(qkv_s, c0, c1, c2)


def _gather_body(idx_ref, src_ref, *rest, k, tm, n_out, combine, unroll):
    if combine:
        base_ref, wt_ref, o_ref, buf, sem = rest
    else:
        o_ref, buf, sem = rest
    i = pl.program_id(0)

    def copy(j, r):
        row = idx_ref[j * n_out + i * tm + r]
        return pltpu.make_async_copy(src_ref.at[pl.ds(row, 1), :], buf.at[j, pl.ds(r, 1), :], sem.at[j])

    def issue(pair, carry):
        for half in range(2):
            for j in range(k):
                copy(j, 2 * pair + half).start(priority=(half + j) % 2)
        return carry

    def wait(r, carry):
        for j in range(k):
            copy(j, r).wait()
        return carry

    lax.fori_loop(0, tm // 2, issue, 0, unroll=unroll // 2)
    lax.fori_loop(0, tm, wait, 0, unroll=unroll)
    if combine:
        acc = base_ref[...]
        for j in range(k):
            acc = acc + buf[j] * wt_ref[:, j:j + 1]
    else:
        acc = buf[0]
        for j in range(1, k):
            acc = acc + buf[j]
    o_ref[...] = acc.astype(o_ref.dtype)


def _gather_rows(src, idx, out_dtype, base=None, weights=None):
    k, p = idx.shape
    tm = GATHER_TILE
    w = src.shape[1]
    in_specs = [pl.BlockSpec(memory_space=pl.ANY)]
    args = [src]
    if base is not None:
        in_specs += [pl.BlockSpec((tm, w), lambda i, idx_ref: (i, 0)),
                     pl.BlockSpec((tm, k), lambda i, idx_ref: (i, 0))]
        args += [base, weights]
    return pl.pallas_call(
        functools.partial(_gather_body, k=k, tm=tm, n_out=p, combine=base is not None, unroll=8),
        grid_spec=pltpu.PrefetchScalarGridSpec(
            num_scalar_prefetch=1,
            grid=(p // tm,),
            in_specs=in_specs,
            out_specs=pl.BlockSpec((tm, w), lambda i, idx_ref: (i, 0)),
            scratch_shapes=[pltpu.VMEM((k, tm, w), F32), pltpu.SemaphoreType.DMA((k,))]),
        out_shape=jax.ShapeDtypeStruct((p, w), out_dtype),
        compiler_params=_cparams("arbitrary"),
        name="gather_rows",
    )(idx.reshape(k * p), *args)


def _experts_body(te_ref, nu_ref, x_ref, wg_ref, wu_ref, wd_ref, *rest, has_prev):
    if has_prev:
        prev_ref, o_ref, wg_scr, wu_scr, wd_scr = rest
    else:
        o_ref, wg_scr, wu_scr, wd_scr = rest
    t = pl.program_id(0)

    @pl.when(jnp.logical_or(t == 0, te_ref[t] != te_ref[jnp.maximum(t - 1, 0)]))
    def _():
        wg_scr[...] = wg_ref[0, 0].astype(BF16)
        wu_scr[...] = wu_ref[0, 0].astype(BF16)
        wd_scr[...] = wd_ref[0, 0].astype(BF16)

    @pl.when(t < nu_ref[0])
    def _():
        x = x_ref[...]
        a = jnp.dot(x, wg_scr[...], preferred_element_type=F32)
        b = jnp.dot(x, wu_scr[...], preferred_element_type=F32)
        hid = a * (1.0 / (1.0 + jnp.exp(-a))) * b
        y = jnp.dot(hid.astype(BF16), wd_scr[...], preferred_element_type=F32)
        if has_prev:
            y = y + prev_ref[...].astype(F32)
        o_ref[...] = y.astype(o_ref.dtype)

    @pl.when(t >= nu_ref[0])
    def _():
        o_ref[...] = jnp.zeros(o_ref.shape, o_ref.dtype)


def _experts(xs, tile_expert, n_used, w_gate, w_up, w_down, layer):
    p, d = xs.shape
    tm = MOE_TILE
    row = lambda t, te, nu: (t, 0)
    y = None
    for f in range(D_FF // FF_BLK):
        in_specs = [pl.BlockSpec((tm, d), row),
                    pl.BlockSpec((1, 1, d, FF_BLK), lambda t, te, nu, f=f: (layer, te[t], 0, f)),
                    pl.BlockSpec((1, 1, d, FF_BLK), lambda t, te, nu, f=f: (layer, te[t], 0, f)),
                    pl.BlockSpec((1, 1, FF_BLK, d), lambda t, te, nu, f=f: (layer, te[t], f, 0))]
        args = [xs, w_gate, w_up, w_down]
        if y is not None:
            in_specs.append(pl.BlockSpec((tm, d), row))
            args.append(y)
        y = pl.pallas_call(
            functools.partial(_experts_body, has_prev=y is not None),
            grid_spec=pltpu.PrefetchScalarGridSpec(
                num_scalar_prefetch=2,
                grid=(p // tm,),
                in_specs=in_specs,
                out_specs=pl.BlockSpec((tm, d), row),
                scratch_shapes=[pltpu.VMEM((d, FF_BLK), BF16), pltpu.VMEM((d, FF_BLK), BF16),
                                pltpu.VMEM((FF_BLK, d), BF16)]),
            out_shape=jax.ShapeDtypeStruct((p, d), F32 if f == D_FF // FF_BLK - 1 else BF16),
            compiler_params=_cparams("arbitrary"),
            name="experts",
        )(tile_expert, n_used, *args)
    return y


def _dispatch_plan(e_idx, n_rows):
    tm = MOE_TILE
    n_pair = e_idx.shape[0] * 2
    flat_e = e_idx.reshape(-1)
    onehot = (flat_e[:, None] == jnp.arange(N_EXPERTS, dtype=jnp.int32)[None, :]).astype(jnp.int32)
    incl = jnp.cumsum(onehot, axis=0)
    rank = jnp.sum((incl - onehot) * onehot, axis=1)
    counts = incl[-1]
    padded = ((counts + tm - 1) // tm) * tm
    ends = jnp.cumsum(padded)
    starts = ends - padded
    pos = jnp.sum(onehot * starts[None, :], axis=1) + rank
    filler = jnp.arange(n_rows, dtype=jnp.int32) % e_idx.shape[0]
    row_token = filler.at[pos].set(jnp.arange(n_pair, dtype=jnp.int32) // 2)
    n_tiles = n_rows // tm
    tile_start = jnp.arange(n_tiles, dtype=jnp.int32) * tm
    tile_expert = jnp.sum((tile_start[:, None] >= ends[None, :]).astype(jnp.int32), axis=1)
    n_used = (ends[-1] // tm).astype(jnp.int32)
    last_e = jnp.max(jnp.where(counts > 0, jnp.arange(N_EXPERTS, dtype=jnp.int32), 0))
    tile_expert = jnp.minimum(tile_expert, last_e).astype(jnp.int32)
    return pos.reshape(-1, 2), row_token, tile_expert, n_used.reshape(1)


def _moe(x, g_norm, w_router, w_gate, w_up, w_down, layer):
    n, d = x.shape
    xn, route = _norm_route(x, g_norm, w_router)
    e_idx = route[:, :2].astype(jnp.int32)
    w_sel = route[:, 2:4]
    n_rows = ((2 * n + N_EXPERTS * (MOE_TILE - 1)) // MOE_TILE + 1) * MOE_TILE
    pos, row_token, tile_expert, n_used = _dispatch_plan(e_idx, n_rows)
    xs = _gather_rows(xn, row_token.reshape(1, n_rows), BF16)
    ys = _experts(xs, tile_expert, n_used, w_gate, w_up, w_down, layer)
    return _gather_rows(ys, jnp.transpose(pos), F32, base=x, weights=w_sel)


def _rope_tables(bp, tp, bs, ts):
    half = B_DH // 2
    inv_freq = ROPE_THETA ** (-jnp.arange(half, dtype=F32) / half)
    pos = jnp.concatenate([jnp.tile(jnp.arange(tp, dtype=jnp.int32), bp),
                           jnp.tile(PAST_LEN + jnp.arange(ts, dtype=jnp.int32), bs)])
    ang = pos.astype(F32)[:, None] * inv_freq[None, :]
    cos = jnp.cos(ang)
    sin = jnp.sin(ang)
    return jnp.concatenate([cos, cos], axis=1), jnp.concatenate([-sin, sin], axis=1)


def kernel(x_prompt, x_sample, state_mlstm_C, state_mlstm_n, state_mlstm_m, cache_kv_w128, cache_kv_w512, cache_kv_w2048, g_norm_mix, g_norm_ffn, g_norm_final, w_in_a, b_gates_a, g_head_a, w_out_a, w_qkv_b, w_out_b, w_router_coarse, w_router_fine, w_gate_e, w_up_e, w_down_e):
    bp, tp, d = x_prompt.shape
    bs, ts, _ = x_sample.shape
    n_p, n_s = bp * tp, bs * ts
    x = jnp.concatenate([x_prompt.reshape(n_p, d), x_sample.reshape(n_s, d)], axis=0)

    def router_w(i):
        w = jnp.concatenate([w_router_coarse[i], w_router_fine[i]], axis=1)
        return jnp.pad(w, ((0, 0), (0, LANE - w.shape[1])))

    xn = _rmsnorm(x, g_norm_mix[0], BF16)
    w_in = w_in_a[0]
    proj = _matmul(xn, w_in, A_MAIN_W, 1024, BF16, tm=MM_TILE)
    w_g = jnp.pad(w_in[:, A_MAIN_W:], ((0, 0), (0, LANE - 2 * A_HEADS)))
    gates = _matmul(xn, w_g, LANE, LANE, F32)
    bias = jnp.pad(b_gates_a[0], (0, LANE - 2 * A_HEADS))
    bcol = bias.reshape(1, LANE)
    brow = jnp.broadcast_to(b_gates_a[0][:, None], (2 * A_HEADS, LANE))
    g_head = g_head_a[0].reshape(1, A_V_W)

    def gate_rows(gc, bsz, t):
        return jnp.transpose(gc.reshape(bsz, t, LANE)[:, :, :2 * A_HEADS], (0, 2, 1))

    zeros_c = jnp.zeros((bp, A_HEADS, A_DK, A_DV), F32)
    zeros_n = jnp.zeros((bp, A_HEADS, A_DK), F32)
    zeros_m = jnp.zeros((bp, A_HEADS, LANE), F32)
    h_p, c_p, nn_p, m_p = _mlstm(proj, gates, gate_rows(gates[:n_p], bp, tp), bcol, brow, g_head,
                                 zeros_c, zeros_n, zeros_m, A_CHUNK_PROMPT, tp)
    ts_pad = 8

    def pad_t(a):
        a = jnp.pad(a.reshape(bs, ts, a.shape[-1]), ((0, 0), (0, ts_pad - ts), (0, 0)))
        return a.reshape(bs * ts_pad, a.shape[-1])

    gc_s = pad_t(gates[n_p:])
    m0_s = jnp.broadcast_to(state_mlstm_m[0][:, :, None], (bs, A_HEADS, LANE))
    h_s, c_s, nn_s, m_s = _mlstm(pad_t(proj[n_p:].astype(F32)), gc_s, gate_rows(gc_s, bs, ts_pad), bcol, brow,
                                 g_head, state_mlstm_C[0], state_mlstm_n[0], m0_s, ts_pad, ts)
    h_s = h_s.reshape(bs, ts_pad, A_V_W)[:, :ts].reshape(n_s, A_V_W).astype(BF16)
    h_all = jnp.concatenate([h_p, h_s], axis=0)
    x = _matmul(h_all, w_out_a[0], d, 1024, F32, res=x)
    x = _moe(x, g_norm_ffn[0], router_w(0), w_gate_e, w_up_e, w_down_e, 0)

    ng = len(B_GROUPS)
    xn = _rmsnorm(x, g_norm_mix[1], BF16)
    cos, sin = _rope_tables(bp, tp, bs, ts)
    qkvh = _matmul(xn, w_qkv_b[0], B_QKV_W, 1024, F32, rot=(2 * ng * B_HW // 1024, cos, sin), tm=MM_TILE)
    attn_p = _prompt_attention(qkvh, _band_bias_tiles(), bp, tp)
    qkv_s = jnp.transpose(qkvh[:, n_p:], (1, 0, 2)).reshape(bs, ts, 3 * ng, B_HEADS, B_DH)
    caches = (cache_kv_w128, cache_kv_w512, cache_kv_w2048)
    cache_views = [c.reshape((1, bs, B_STEPS) + ((dil,) if dil > 1 else ()) + (2, B_HEADS, B_DH))
                   for c, (_, dil) in zip(caches, B_GROUPS)]
    attn_s = _sample_attention(qkv_s, *cache_views)
    attn = jnp.concatenate([attn_p, attn_s.reshape(n_s, B_HW)], axis=0)
    x = _matmul(attn, w_out_b[0], d, 1024, F32, res=x)
    x = _moe(x, g_norm_ffn[1], router_w(1), w_gate_e, w_up_e, w_down_e, 1)

    y_prompt = _rmsnorm(x, g_norm_final, F32, 0, n_p // ROW_TILE).reshape(bp, tp, d)
    y_sample = _rmsnorm(x, g_norm_final, F32, n_p // ROW_TILE, n_s // ROW_TILE).reshape(bs, ts, d)

    qh = qkvh.reshape(3, ng, B_HEADS, n_p + n_s, B_DH)

    def prompt_rows(g, keep):
        per_b = [qh[1:, g, :, (b + 1) * tp - keep:(b + 1) * tp] for b in range(bp)]
        return jnp.transpose(jnp.stack(per_b), (0, 3, 1, 2, 4))[None]

    caches_p = [prompt_rows(g, min(win, tp)) for g, (win, _) in enumerate(B_GROUPS)]
    caches_s = [jnp.stack([qkv_s[:, :, ng + g], qkv_s[:, :, 2 * ng + g]], axis=2)[None] for g in range(ng)]
    return (y_prompt, y_sample,
            c_p[None], nn_p[None], m_p[:, :, 0][None],
            c_s[None], nn_s[None], m_s[:, :, 0][None],
            caches_p[0], caches_p[1], caches_p[2],
            caches_s[0], caches_s[1], caches_s[2])
```
